```python
import math
import jax, jax.numpy as jnp
from jax import lax
import numpy as np

D_MODEL = 1024
BATCH = 1
SEQ = 16384
DEPTH = 1
DEC_BATCH = 32
DEC_SEQ = 64
PAST_LEN = 2048

CHUNK = 64
Q_BLOCK = 128
N_HEADS = 8
HEAD_DIM = 64
ATTN_DIM = N_HEADS * HEAD_DIM
IDX_HEADS = 8
IDX_DIM = 64
IDX_W_SCALE = (IDX_HEADS * IDX_DIM) ** -0.5
TOPK_MAX = 256
ROPE_THETA = 500000.0
ROT_FRACTION = 4
CONV_CH = 512
CONV_WIDTH = 31
D_FF = 2816
EPS = 1e-6

SPLITS = (IDX_HEADS * IDX_DIM, IDX_HEADS, IDX_DIM, ATTN_DIM, ATTN_DIM, ATTN_DIM, 2 * CONV_CH, 2 * D_MODEL)
D_IN = sum(SPLITS)

kernel_name = 'dsa_conformer_hybrid_stream_step'


def rms_norm(x, g):
    xf = x.astype(jnp.float32)
    y = xf * lax.rsqrt(jnp.mean(xf * xf, axis=-1, keepdims=True) + EPS)
    return (y * g.astype(jnp.float32)).astype(x.dtype)


def layer_norm(x, g, b):
    xf = x.astype(jnp.float32)
    mu = jnp.mean(xf, axis=-1, keepdims=True)
    xc = xf - mu
    var = jnp.mean(xc * xc, axis=-1, keepdims=True)
    return (xc * lax.rsqrt(var + EPS) * g.astype(jnp.float32) + b.astype(jnp.float32)).astype(x.dtype)


def ffn_half(x, pre_g, w_gate, w_up, w_down, post_g):
    h = rms_norm(x, pre_g)
    f = (jax.nn.silu(h @ w_gate) * (h @ w_up)) @ w_down
    return x + 0.5 * rms_norm(f, post_g)


def rope_partial(x, pos):
    rot = x.shape[-1] // ROT_FRACTION
    half = rot // 2
    inv_freq = jnp.exp(-math.log(ROPE_THETA) * jnp.arange(half, dtype=jnp.float32) * (2.0 / rot))
    ang = pos.astype(jnp.float32)[:, None] * inv_freq[None, :]
    cos = jnp.cos(ang)[:, None, :]
    sin = jnp.sin(ang)[:, None, :]
    x1 = x[..., :half].astype(jnp.float32)
    x2 = x[..., half:rot].astype(jnp.float32)
    r = jnp.concatenate([x1 * cos - x2 * sin, x2 * cos + x1 * sin], axis=-1).astype(x.dtype)
    return jnp.concatenate([r, x[..., rot:]], axis=-1)


def mixer_inputs(n, pos, w_in, b_gate):
    B, T, _ = n.shape
    proj = n @ w_in
    parts = []
    o = 0
    for width in SPLITS:
        parts.append(proj[..., o:o + width])
        o += width
    iq_, iw_, ik_, q_, k_, v_, glu_, gate_ = parts
    iq = rope_partial(iq_.reshape(B, T, IDX_HEADS, IDX_DIM), pos)
    iw = iw_ * IDX_W_SCALE
    ik = rope_partial(ik_[:, :, None, :], pos)[:, :, 0, :]
    q = rope_partial(q_.reshape(B, T, N_HEADS, HEAD_DIM), pos)
    k = rope_partial(k_.reshape(B, T, N_HEADS, HEAD_DIM), pos)
    v = v_.reshape(B, T, N_HEADS, HEAD_DIM)
    u = glu_[..., :CONV_CH] * jax.nn.sigmoid(glu_[..., CONV_CH:])
    g = jax.nn.sigmoid(gate_ + b_gate)
    return iq, iw, ik, q, k, v, u, g[..., :D_MODEL], g[..., D_MODEL:]


def sparse_attend(q, iq, iw, k_all, v_all, ik_all, limit, topk):
    B, Tq = q.shape[:2]
    S = k_all.shape[1]
    dots = jnp.einsum('bqhd,bsd->bqhs', iq, ik_all).astype(jnp.float32)
    iscore = jnp.einsum('bqh,bqhs->bqs', iw.astype(jnp.float32), jax.nn.relu(dots))
    admissible = jnp.arange(S, dtype=jnp.int32)[None, :] < limit[:, None]
    iscore = jnp.where(admissible[None], iscore, -jnp.inf)
    _, idx = lax.top_k(iscore, topk)
    valid = idx < limit[None, :, None]
    k_sel = jax.vmap(lambda kb, ib: kb[ib])(k_all, idx)
    v_sel = jax.vmap(lambda vb, ib: vb[ib])(v_all, idx)
    s = jnp.einsum('bqhd,bqkhd->bhqk', q, k_sel).astype(jnp.float32) * (HEAD_DIM ** -0.5)
    s = jnp.where(valid[:, None], s, -jnp.inf)
    p = jax.nn.softmax(s, axis=-1).astype(v_sel.dtype)
    o = jnp.einsum('bhqk,bqkhd->bqhd', p, v_sel)
    return o.reshape(B, Tq, ATTN_DIM)


def prompt_attention(q, iq, iw, k, v, ik, topk):
    B, L = q.shape[:2]
    n_blocks = L // Q_BLOCK

    def block(i):
        s0 = i * Q_BLOCK
        sl = lambda a: lax.dynamic_slice_in_dim(a, s0, Q_BLOCK, axis=1)
        t = s0 + jnp.arange(Q_BLOCK, dtype=jnp.int32)
        limit = (t // CHUNK + 1) * CHUNK
        return sparse_attend(sl(q), sl(iq), sl(iw), k, v, ik, limit, topk)

    out = lax.map(block, jnp.arange(n_blocks, dtype=jnp.int32))
    return out.transpose(1, 0, 2, 3).reshape(B, L, ATTN_DIM)


def depthwise_causal(u_padded, w, b):
    y = lax.conv_general_dilated(u_padded, w[:, None, :], window_strides=(1,), padding='VALID',
                                 dimension_numbers=('NWC', 'WIO', 'NWC'), feature_group_count=CONV_CH)
    return y + b


def branch_merge(a, c_dw, ga, gc, w_attn_o, ln_g, ln_b, w_conv_o, w_out):
    a_proj = a @ w_attn_o
    c_proj = jax.nn.silu(layer_norm(c_dw, ln_g, ln_b)) @ w_conv_o
    return (ga * a_proj + gc * c_proj) @ w_out


def setup_inputs(seed: int = 0) -> dict:
    key = jax.random.key(seed)
    ks = jax.random.split(key, 32)

    def nrm(k, shape, scale=1.0):
        return jax.random.normal(k, shape, jnp.float32) * scale

    def gain(k, shape):
        return 1.0 + 0.05 * nrm(k, shape)

    L = DEPTH
    return {
        'x_prompt': nrm(ks[0], (BATCH, SEQ, D_MODEL)),
        'x_sample': nrm(ks[1], (DEC_BATCH, DEC_SEQ, D_MODEL)),
        'cache_k': nrm(ks[2], (L, DEC_BATCH, PAST_LEN, N_HEADS, HEAD_DIM)),
        'cache_v': nrm(ks[3], (L, DEC_BATCH, PAST_LEN, N_HEADS, HEAD_DIM)),
        'cache_idx_k': nrm(ks[4], (L, DEC_BATCH, PAST_LEN, IDX_DIM)),
        'state_conv': nrm(ks[5], (L, DEC_BATCH, CONV_WIDTH - 1, CONV_CH), 0.5),
        'ffn1_pre_g': gain(ks[6], (L, D_MODEL)),
        'ffn1_w_gate': nrm(ks[7], (L, D_MODEL, D_FF), D_MODEL ** -0.5),
        'ffn1_w_up': nrm(ks[8], (L, D_MODEL, D_FF), D_MODEL ** -0.5),
        'ffn1_w_down': nrm(ks[9], (L, D_FF, D_MODEL), D_FF ** -0.5),
        'ffn1_post_g': gain(ks[10], (L, D_MODEL)),
        'mix_pre_g': gain(ks[11], (L, D_MODEL)),
        'w_in': nrm(ks[12], (L, D_MODEL, D_IN), D_MODEL ** -0.5),
        'b_gate': nrm(ks[13], (L, 2 * D_MODEL), 0.02),
        'w_attn_o': nrm(ks[14], (L, ATTN_DIM, D_MODEL), ATTN_DIM ** -0.5),
        'conv_dw_w': nrm(ks[15], (L, CONV_WIDTH, CONV_CH), CONV_WIDTH ** -0.5),
        'conv_dw_b': nrm(ks[16], (L, CONV_CH), 0.02),
        'conv_ln_g': gain(ks[17], (L, CONV_CH)),
        'conv_ln_b': nrm(ks[18], (L, CONV_CH), 0.02),
        'w_conv_o': nrm(ks[19], (L, CONV_CH, D_MODEL), CONV_CH ** -0.5),
        'w_out': nrm(ks[20], (L, D_MODEL, D_MODEL), D_MODEL ** -0.5),
        'mix_post_g': gain(ks[21], (L, D_MODEL)),
        'ffn2_pre_g': gain(ks[22], (L, D_MODEL)),
        'ffn2_w_gate': nrm(ks[23], (L, D_MODEL, D_FF), D_MODEL ** -0.5),
        'ffn2_w_up': nrm(ks[24], (L, D_MODEL, D_FF), D_MODEL ** -0.5),
        'ffn2_w_down': nrm(ks[25], (L, D_FF, D_MODEL), D_FF ** -0.5),
        'ffn2_post_g': gain(ks[26], (L, D_MODEL)),
    }


def reference(x_prompt, x_sample, cache_k, cache_v, cache_idx_k, state_conv,
              ffn1_pre_g, ffn1_w_gate, ffn1_w_up, ffn1_w_down, ffn1_post_g,
              mix_pre_g, w_in, b_gate, w_attn_o, conv_dw_w, conv_dw_b, conv_ln_g, conv_ln_b,
              w_conv_o, w_out, mix_post_g,
              ffn2_pre_g, ffn2_w_gate, ffn2_w_up, ffn2_w_down, ffn2_post_g):
    Lp = x_prompt.shape[1]
    Ts = x_sample.shape[1]
    P = cache_k.shape[2]
    pos_p = jnp.arange(Lp, dtype=jnp.int32)
    pos_s = P + jnp.arange(Ts, dtype=jnp.int32)
    topk_p = min(TOPK_MAX, Lp // 4)
    topk_s = min(TOPK_MAX, (P + Ts) // 4)
    limit_s = jnp.full((Ts,), P + Ts, dtype=jnp.int32)

    hp, hs = x_prompt, x_sample
    kp_l, vp_l, ikp_l, cp_l, ks_l, vs_l, iks_l, cs_l = [], [], [], [], [], [], [], []
    for d in range(DEPTH):
        ffn1 = (ffn1_pre_g[d], ffn1_w_gate[d], ffn1_w_up[d], ffn1_w_down[d], ffn1_post_g[d])
        ffn2 = (ffn2_pre_g[d], ffn2_w_gate[d], ffn2_w_up[d], ffn2_w_down[d], ffn2_post_g[d])
        merge_w = (w_attn_o[d], conv_ln_g[d], conv_ln_b[d], w_conv_o[d], w_out[d])

        hp = ffn_half(hp, *ffn1)
        hs = ffn_half(hs, *ffn1)

        iq, iw, ik, q, k, v, u, ga, gc = mixer_inputs(rms_norm(hp, mix_pre_g[d]), pos_p, w_in[d], b_gate[d])
        a = prompt_attention(q, iq, iw, k, v, ik, topk_p)
        c_dw = depthwise_causal(jnp.pad(u, ((0, 0), (CONV_WIDTH - 1, 0), (0, 0))), conv_dw_w[d], conv_dw_b[d])
        hp = hp + rms_norm(branch_merge(a, c_dw, ga, gc, *merge_w), mix_post_g[d])
        kp_l.append(k)
        vp_l.append(v)
        ikp_l.append(ik)
        cp_l.append(u[:, Lp - (CONV_WIDTH - 1):])

        iq, iw, ik, q, k, v, u, ga, gc = mixer_inputs(rms_norm(hs, mix_pre_g[d]), pos_s, w_in[d], b_gate[d])
        k_all = jnp.concatenate([cache_k[d], k], axis=1)
        v_all = jnp.concatenate([cache_v[d], v], axis=1)
        ik_all = jnp.concatenate([cache_idx_k[d], ik], axis=1)
        a = sparse_attend(q, iq, iw, k_all, v_all, ik_all, limit_s, topk_s)
        u_all = jnp.concatenate([state_conv[d], u], axis=1)
        c_dw = depthwise_causal(u_all, conv_dw_w[d], conv_dw_b[d])
        hs = hs + rms_norm(branch_merge(a, c_dw, ga, gc, *merge_w), mix_post_g[d])
        ks_l.append(k)
        vs_l.append(v)
        iks_l.append(ik)
        cs_l.append(u_all[:, Ts:])

        hp = ffn_half(hp, *ffn2)
        hs = ffn_half(hs, *ffn2)

    return (hp, hs,
            jnp.stack(kp_l, 0), jnp.stack(vp_l, 0), jnp.stack(ikp_l, 0), jnp.stack(cp_l, 0),
            jnp.stack(ks_l, 0), jnp.stack(vs_l, 0), jnp.stack(iks_l, 0), jnp.stack(cs_l, 0))
```

```python
import functools
import math

import jax
import jax.numpy as jnp
from jax import lax
from jax.experimental import pallas as pl
from jax.experimental.pallas import tpu as pltpu

F32 = jnp.float32
BF16 = jnp.bfloat16
I32 = jnp.int32

CHUNK = 64
N_HEADS = 8
HEAD_DIM = 64
ATTN_DIM = N_HEADS * HEAD_DIM
IDX_HEADS = 8
IDX_DIM = 64
IDX_W_SCALE = (IDX_HEADS * IDX_DIM) ** -0.5
TOPK_MAX = 256
ROPE_THETA = 500000.0
ROT = HEAD_DIM // 4
CONV_CH = 512
CONV_WIDTH = 31
HIST = CONV_WIDTH - 1
EPS = 1e-6

LANES = 128
VMEM_LIMIT = 56 * 1024 * 1024
INT_MIN = -(2 ** 31)
NEG_BIG = -1e30

_NT = (((1,), (1,)), ((), ()))


def _rms(x, g):
    return x * lax.rsqrt(jnp.mean(x * x, axis=-1, keepdims=True) + EPS) * g


def _sigmoid(x):
    return 1.0 / (1.0 + jnp.exp(-x))


def _const_spec(shape):
    nd = len(shape)
    return pl.BlockSpec(shape, lambda *_: (0,) * nd, pipeline_mode=pl.Buffered(1))


def _ffn_kernel(x_ref, pre_g_ref, wg_ref, wu_ref, wd_ref, post_g_ref, o_ref):
    x = x_ref[...]
    h = _rms(x, pre_g_ref[...]).astype(BF16)
    g = jnp.dot(h, wg_ref[...], preferred_element_type=F32)
    u = jnp.dot(h, wu_ref[...], preferred_element_type=F32)
    a = (g * _sigmoid(g) * u).astype(BF16)
    f = jnp.dot(a, wd_ref[...], preferred_element_type=F32)
    o_ref[...] = x + 0.5 * _rms(f, post_g_ref[...])


def _ffn_half(x, pre_g, wg, wu, wd, post_g, tm=256):
    t, d = x.shape
    dff = wg.shape[1]
    row = pl.BlockSpec((tm, d), lambda i: (i, 0))
    return pl.pallas_call(
        _ffn_kernel,
        out_shape=jax.ShapeDtypeStruct((t, d), F32),
        grid=(t // tm,),
        in_specs=[row, _const_spec((1, d)), _const_spec((d, dff)), _const_spec((d, dff)),
                  _const_spec((dff, d)), _const_spec((1, d))],
        out_specs=row,
        compiler_params=pltpu.CompilerParams(dimension_semantics=("parallel",),
                                             vmem_limit_bytes=VMEM_LIMIT),
        name="ffn_half",
    )(x, pre_g, wg, wu, wd, post_g)


def _rope_tables(pos):
    half = ROT // 2
    inv_freq = jnp.exp(-math.log(ROPE_THETA) * jnp.arange(half, dtype=F32) * (2.0 / ROT))
    ang = pos.astype(F32)[:, None] * inv_freq[None, :]
    cos, sin = jnp.cos(ang), jnp.sin(ang)
    t = pos.shape[0]
    ones = jnp.ones((t, HEAD_DIM - ROT), F32)
    c = jnp.concatenate([cos, cos, ones], axis=1)
    s_up = jnp.concatenate([-sin, jnp.zeros((t, HEAD_DIM - half), F32)], axis=1)
    s_dn = jnp.concatenate([jnp.zeros((t, half), F32), sin, jnp.zeros((t, HEAD_DIM - ROT), F32)], axis=1)
    dup = lambda a: jnp.concatenate([a, a], axis=1)
    return dup(c), dup(s_up), dup(s_dn)


def _proj_kernel(x_ref, g_ref, w_ref, c_ref, sup_ref, sdn_ref,
                 iq_o, q_o, kf_o, kb_o, vf_o, vb_o, ikf_o, ik2_o, iw_o):
    n = _rms(x_ref[...], g_ref[...]).astype(BF16)
    p = jnp.dot(n, w_ref[...], preferred_element_type=F32)
    c, s_up, s_dn = c_ref[...], sup_ref[...], sdn_ref[...]

    def rope(x):
        return x * c + pltpu.roll(x, LANES - ROT // 2, 1) * s_up + pltpu.roll(x, ROT // 2, 1) * s_dn

    a = ATTN_DIM
    for gidx in range(a // LANES):
        sl = slice(gidx * LANES, (gidx + 1) * LANES)
        iq_o[:, sl] = rope(p[:, sl]).astype(BF16)
        q_o[:, sl] = (rope(p[:, a + gidx * LANES:a + (gidx + 1) * LANES]) * (HEAD_DIM ** -0.5)).astype(BF16)
        k = rope(p[:, 2 * a + gidx * LANES:2 * a + (gidx + 1) * LANES])
        kf_o[:, sl] = k
        kb_o[:, sl] = k.astype(BF16)
        v = p[:, 3 * a + gidx * LANES:3 * a + (gidx + 1) * LANES]
        vf_o[:, sl] = v
        vb_o[:, sl] = v.astype(BF16)
    ik = rope(p[:, 4 * a:4 * a + LANES])
    ikf_o[...] = ik[:, :IDX_DIM]
    ik2_o[...] = ik.astype(BF16)
    iw_o[...] = p[:, 4 * a + LANES:4 * a + 2 * LANES] * IDX_W_SCALE


def _mixer_proj(x, g, w2, tables, tm=256):
    t, d = x.shape
    ncol = w2.shape[1]
    row = lambda width: pl.BlockSpec((tm, width), lambda i: (i, 0))
    a = ATTN_DIM
    outs = [
        jax.ShapeDtypeStruct((t, a), BF16),
        jax.ShapeDtypeStruct((t, a), BF16),
        jax.ShapeDtypeStruct((t, a), F32),
        jax.ShapeDtypeStruct((t, a), BF16),
        jax.ShapeDtypeStruct((t, a), F32),
        jax.ShapeDtypeStruct((t, a), BF16),
        jax.ShapeDtypeStruct((t, IDX_DIM), F32),
        jax.ShapeDtypeStruct((t, LANES), BF16),
        jax.ShapeDtypeStruct((t, LANES), F32),
    ]
    return pl.pallas_call(
        _proj_kernel,
        out_shape=outs,
        grid=(t // tm,),
        in_specs=[row(d), _const_spec((1, d)), _const_spec((d, ncol)), row(LANES), row(LANES), row(LANES)],
        out_specs=[row(a), row(a), row(a), row(a), row(a), row(a), row(IDX_DIM), row(LANES), row(LANES)],
        compiler_params=pltpu.CompilerParams(dimension_semantics=("parallel",),
                                             vmem_limit_bytes=VMEM_LIMIT),
        name="mixer_proj",
    )(x, g, w2, *tables)


def _sortable(x):
    b = lax.bitcast_convert_type(x, I32)
    k = b ^ ((b >> 31) & 0x7FFFFFFF)
    return jnp.where(x == 0.0, 0, k)


def _head_lhs(x, h):
    grp, odd = divmod(h, 2)
    xg = x[:, grp * LANES:(grp + 1) * LANES].astype(F32)
    lane = lax.broadcasted_iota(I32, xg.shape, 1)
    keep = (lane >= HEAD_DIM) if odd else (lane < HEAD_DIM)
    return jnp.where(keep, xg, 0.0).astype(BF16)


def _index_scores(iqz_ref, w, ikc):
    acc = None
    for h in range(IDX_HEADS):
        d = lax.dot_general(iqz_ref[h], ikc, _NT, preferred_element_type=F32)
        term = w[:, h:h + 1] * jnp.maximum(d, 0.0)
        acc = term if acc is None else acc + term
    return acc


def _lane_fold(x):
    out = x[:, :LANES]
    for l in range(1, x.shape[1] // LANES):
        out = out + x[:, l * LANES:(l + 1) * LANES]
    return out


def _select_threshold(count_ge, rows, topk):
    def step(i, t):
        cand = t + jnp.left_shift(jnp.int32(1), 31 - i)
        cnt = count_ge(cand)
        return jnp.where(cnt >= topk, cand, t)
    t = lax.fori_loop(0, 32, step, jnp.full((rows, 1), INT_MIN, I32))
    return jnp.maximum(t, INT_MIN + 1)


def _bias_plain(key, t):
    return jnp.where(key >= t, 0.0, NEG_BIG)


def _bias_ties(key, t, need, run):
    sc = key.shape[1]
    eq = key == t
    eqf = jnp.where(eq, 1.0, 0.0)
    r = lax.broadcasted_iota(I32, (sc, sc), 0)
    c = lax.broadcasted_iota(I32, (sc, sc), 1)
    tri = jnp.where(r < c, 1.0, 0.0).astype(BF16)
    before = run + jnp.dot(eqf.astype(BF16), tri, preferred_element_type=F32)
    sel = (key > t) | (eq & (before < need))
    return jnp.where(sel, 0.0, NEG_BIG), run + jnp.sum(eqf, axis=1, keepdims=True)


def _attend_chunk(qz_ref, bias, kc, vc, m_ref, l_ref, acc_ref):
    for h in range(N_HEADS):
        grp = h // 2
        sl = slice(grp * LANES, (grp + 1) * LANES)
        s = lax.dot_general(qz_ref[h], kc[:, sl], _NT, preferred_element_type=F32) + bias
        m_prev = m_ref[h][:, :1]
        m_next = jnp.maximum(m_prev, jnp.max(s, axis=1, keepdims=True))
        alpha = jnp.exp(m_prev - m_next)
        p = jnp.exp(s - m_next)
        l_ref[h] = alpha * l_ref[h] + jnp.sum(p, axis=1, keepdims=True)
        pv = jnp.dot(p.astype(BF16), vc[:, sl], preferred_element_type=F32)
        acc_ref[h] = alpha * acc_ref[h] + pv
        m_ref[h] = jnp.broadcast_to(m_next, m_ref.shape[1:])


def _init_attend(m_ref, l_ref, acc_ref):
    m_ref[...] = jnp.full(m_ref.shape, NEG_BIG, F32)
    l_ref[...] = jnp.zeros(l_ref.shape, F32)
    acc_ref[...] = jnp.zeros(acc_ref.shape, F32)


def _finish_attend(o_ref, l_ref, acc_ref):
    rows = acc_ref.shape[1]
    lane = lax.broadcasted_iota(I32, (rows, LANES), 1)
    for grp in range(N_HEADS // 2):
        lo = acc_ref[2 * grp] / l_ref[2 * grp]
        hi = acc_ref[2 * grp + 1] / l_ref[2 * grp + 1]
        o_ref[:, grp * LANES:(grp + 1) * LANES] = jnp.where(lane < HEAD_DIM, lo, hi).astype(o_ref.dtype)


def _prompt_attn_kernel(q_ref, iq_ref, iw_ref, kb_ref, vb_ref, ik2_ref, o_ref,
                        key_ref, iqz_ref, qz_ref, m_ref, l_ref, acc_ref, *, tq, sc, topk):
    j = pl.program_id(0)
    n_chunks = ((j + 1) * tq + sc - 1) // sc

    iq = iq_ref[...]
    q = q_ref[...]
    for h in range(N_HEADS):
        iqz_ref[h] = _head_lhs(iq, h)
        qz_ref[h] = _head_lhs(q, h)
    w = iw_ref[...]
    row = lax.broadcasted_iota(I32, (tq, 1), 0)
    limit = j * tq + (row // CHUNK + 1) * CHUNK

    def score_chunk(c, carry):
        off = pl.multiple_of(c * sc, sc)
        acc = _index_scores(iqz_ref, w, ik2_ref[pl.ds(off, sc), :])
        sidx = off + lax.broadcasted_iota(I32, (tq, sc), 1)
        key_ref[c] = jnp.where(sidx < limit, _sortable(acc), INT_MIN)
        return carry
    lax.fori_loop(0, n_chunks, score_chunk, 0)

    def count_ge(cand):
        def body(c, acc):
            return acc + _lane_fold(jnp.where(key_ref[c] >= cand, 1.0, 0.0))
        part = lax.fori_loop(0, n_chunks, body, jnp.zeros((tq, LANES), F32))
        return jnp.sum(part, axis=1, keepdims=True)

    t = _select_threshold(count_ge, tq, topk)
    n_ge = count_ge(t)
    has_ties = jnp.max(n_ge) > topk

    @pl.when(jnp.logical_not(has_ties))
    def _():
        def body(c, carry):
            key_ref[c] = lax.bitcast_convert_type(_bias_plain(key_ref[c], t), I32)
            return carry
        lax.fori_loop(0, n_chunks, body, 0)

    @pl.when(has_ties)
    def _():
        need = topk - count_ge(t + 1)
        def body(c, run):
            bias, run = _bias_ties(key_ref[c], t, need, run)
            key_ref[c] = lax.bitcast_convert_type(bias, I32)
            return run
        lax.fori_loop(0, n_chunks, body, jnp.zeros((tq, 1), F32))

    _init_attend(m_ref, l_ref, acc_ref)

    def attend(c, carry):
        off = pl.multiple_of(c * sc, sc)
        bias = lax.bitcast_convert_type(key_ref[c], F32)
        _attend_chunk(qz_ref, bias, kb_ref[pl.ds(off, sc), :], vb_ref[pl.ds(off, sc), :],
                      m_ref, l_ref, acc_ref)
        return carry
    lax.fori_loop(0, n_chunks, attend, 0)
    _finish_attend(o_ref, l_ref, acc_ref)


def _prompt_attention(q, iq, iw, kb, vb, ik2, topk, tq=128, sc=512):
    s = kb.shape[0]
    row = lambda width: pl.BlockSpec((tq, width), lambda j: (j, 0))
    whole = pl.BlockSpec(memory_space=pltpu.VMEM)
    return pl.pallas_call(
        functools.partial(_prompt_attn_kernel, tq=tq, sc=sc, topk=topk),
        out_shape=jax.ShapeDtypeStruct((s, ATTN_DIM), BF16),
        grid=(s // tq,),
        in_specs=[row(ATTN_DIM), row(ATTN_DIM), row(LANES), whole, whole, whole],
        out_specs=row(ATTN_DIM),
        scratch_shapes=[
            pltpu.VMEM((s // sc, tq, sc), I32),
            pltpu.VMEM((N_HEADS, tq, LANES), BF16),
            pltpu.VMEM((N_HEADS, tq, LANES), BF16),
            pltpu.VMEM((N_HEADS, tq, LANES), F32),
            pltpu.VMEM((N_HEADS, tq, LANES), F32),
            pltpu.VMEM((N_HEADS, tq, LANES), F32),
        ],
        compiler_params=pltpu.CompilerParams(dimension_semantics=("arbitrary",),
                                             vmem_limit_bytes=VMEM_LIMIT),
        name="prompt_attention",
    )(q, iq, iw, kb, vb, ik2)


def _sample_attn_kernel(q_ref, iq_ref, iw_ref, kn_ref, vn_ref, ikn_ref, ck_ref, cv_ref, cik_ref, o_ref,
                        key_ref, keyn_ref, iqz_ref, qz_ref, m_ref, l_ref, acc_ref, *, sc, topk):
    ts = q_ref.shape[0]
    past = ck_ref.shape[1]
    n_chunks = past // sc

    iq = iq_ref[...]
    q = q_ref[...]
    for h in range(N_HEADS):
        iqz_ref[h] = _head_lhs(iq, h)
        qz_ref[h] = _head_lhs(q, h)
    w = iw_ref[...]

    pad = lambda x: jnp.concatenate([x, jnp.zeros((LANES - ts, x.shape[1]), x.dtype)], axis=0)
    new_ok = lax.broadcasted_iota(I32, (ts, LANES), 1) < ts

    def cache_ik(c):
        ik = cik_ref[0, c * sc:(c + 1) * sc, :].astype(BF16)
        return jnp.concatenate([ik, ik], axis=1)

    for c in range(n_chunks):
        key_ref[c] = _sortable(_index_scores(iqz_ref, w, cache_ik(c)))
    keyn_ref[...] = jnp.where(new_ok, _sortable(_index_scores(iqz_ref, w, pad(ikn_ref[...]))), INT_MIN)

    def count_ge(cand):
        part = jnp.where(keyn_ref[...] >= cand, 1.0, 0.0)
        for c in range(n_chunks):
            part = part + _lane_fold(jnp.where(key_ref[c] >= cand, 1.0, 0.0))
        return jnp.sum(part, axis=1, keepdims=True)

    t = _select_threshold(count_ge, ts, topk)
    n_ge = count_ge(t)
    has_ties = jnp.max(n_ge) > topk

    @pl.when(jnp.logical_not(has_ties))
    def _():
        for c in range(n_chunks):
            key_ref[c] = lax.bitcast_convert_type(_bias_plain(key_ref[c], t), I32)
        keyn_ref[...] = lax.bitcast_convert_type(_bias_plain(keyn_ref[...], t), I32)

    @pl.when(has_ties)
    def _():
        need = topk - count_ge(t + 1)
        run = jnp.zeros((ts, 1), F32)
        for c in range(n_chunks):
            bias, run = _bias_ties(key_ref[c], t, need, run)
            key_ref[c] = lax.bitcast_convert_type(bias, I32)
        bias, run = _bias_ties(keyn_ref[...], t, need, run)
        keyn_ref[...] = lax.bitcast_convert_type(bias, I32)

    _init_attend(m_ref, l_ref, acc_ref)
    for c in range(n_chunks):
        _attend_chunk(qz_ref, lax.bitcast_convert_type(key_ref[c], F32),
                      ck_ref[0, c * sc:(c + 1) * sc, :].astype(BF16),
                      cv_ref[0, c * sc:(c + 1) * sc, :].astype(BF16), m_ref, l_ref, acc_ref)
    _attend_chunk(qz_ref, lax.bitcast_convert_type(keyn_ref[...], F32),
                  pad(kn_ref[...]), pad(vn_ref[...]), m_ref, l_ref, acc_ref)
    _finish_attend(o_ref, l_ref, acc_ref)


def _sample_attention(q, iq, iw, kb, vb, ik2, cache_k, cache_v, cache_ik, ts, topk, sc=512):
    t = q.shape[0]
    nb, past = cache_k.shape[:2]
    row = lambda width: pl.BlockSpec((ts, width), lambda b: (b, 0))
    cache = lambda width: pl.BlockSpec((1, past, width), lambda b: (b, 0, 0))
    return pl.pallas_call(
        functools.partial(_sample_attn_kernel, sc=sc, topk=topk),
        out_shape=jax.ShapeDtypeStruct((t, ATTN_DIM), BF16),
        grid=(nb,),
        in_specs=[row(ATTN_DIM), row(ATTN_DIM), row(LANES), row(ATTN_DIM), row(ATTN_DIM), row(LANES),
                  cache(ATTN_DIM), cache(ATTN_DIM), cache(IDX_DIM)],
        out_specs=row(ATTN_DIM),
        scratch_shapes=[
            pltpu.VMEM((past // sc, ts, sc), I32),
            pltpu.VMEM((ts, LANES), I32),
            pltpu.VMEM((N_HEADS, ts, LANES), BF16),
            pltpu.VMEM((N_HEADS, ts, LANES), BF16),
            pltpu.VMEM((N_HEADS, ts, LANES), F32),
            pltpu.VMEM((N_HEADS, ts, LANES), F32),
            pltpu.VMEM((N_HEADS, ts, LANES), F32),
        ],
        compiler_params=pltpu.CompilerParams(dimension_semantics=("parallel",),
                                             vmem_limit_bytes=VMEM_LIMIT),
        name="sample_attention",
    )(q, iq, iw, kb, vb, ik2, cache_k, cache_v, cache_ik)


HIST_ROW0 = 2


def _merge_kernel(x_ref, a_ref, hist_ref, g_ref, wglu_ref, wgate_ref, bgate_ref, wa_ref,
                  dww_ref, dwb_ref, lng_ref, lnb_ref, wc_ref, wo_ref, postg_ref,
                  y_ref, conv_o_ref, uext_ref, *, nseq, seqlen, carry):
    i = pl.program_id(0)
    x = x_ref[...]
    n = _rms(x, g_ref[...]).astype(BF16)
    glu = jnp.dot(n, wglu_ref[...], preferred_element_type=F32)
    u = glu[:, :CONV_CH] * _sigmoid(glu[:, CONV_CH:])

    new0 = HIST_ROW0 + HIST
    if carry:
        @pl.when(i == 0)
        def _():
            uext_ref[:, HIST_ROW0:new0, :] = hist_ref[...]

        @pl.when(i > 0)
        def _():
            uext_ref[:, HIST_ROW0:new0, :] = uext_ref[:, seqlen + HIST_ROW0:seqlen + new0, :]
    else:
        uext_ref[:, HIST_ROW0:new0, :] = hist_ref[...]
    uext_ref[:, new0:new0 + seqlen, :] = u.reshape(nseq, seqlen, CONV_CH)
    conv_o_ref[...] = uext_ref[:, seqlen + HIST_ROW0:seqlen + new0, :]

    dww = dww_ref[...]
    cdw = None
    for tap in range(CONV_WIDTH):
        term = uext_ref[:, HIST_ROW0 + tap:HIST_ROW0 + tap + seqlen, :] * dww[tap:tap + 1, :]
        cdw = term if cdw is None else cdw + term
    cdw = cdw.reshape(nseq * seqlen, CONV_CH) + dwb_ref[...]

    mu = jnp.mean(cdw, axis=-1, keepdims=True)
    xc = cdw - mu
    var = jnp.mean(xc * xc, axis=-1, keepdims=True)
    ln = xc * lax.rsqrt(var + EPS) * lng_ref[...] + lnb_ref[...]
    cact = (ln * _sigmoid(ln)).astype(BF16)
    c_proj = jnp.dot(cact, wc_ref[...], preferred_element_type=F32)
    a_proj = jnp.dot(a_ref[...], wa_ref[...], preferred_element_type=F32)

    d = x.shape[1]
    gate = _sigmoid(jnp.dot(n, wgate_ref[...], preferred_element_type=F32) + bgate_ref[...])
    merged = (gate[:, :d] * a_proj + gate[:, d:] * c_proj).astype(BF16)
    y = jnp.dot(merged, wo_ref[...], preferred_element_type=F32)
    y_ref[...] = x + _rms(y, postg_ref[...])


def _branch_merge(x, a, hist, g, wglu, wgate, bgate, wa, dww, dwb, lng, lnb, wc, wo, postg,
                  *, nseq, seqlen, carry):
    t, d = x.shape
    tm = nseq * seqlen
    row = lambda width: pl.BlockSpec((tm, width), lambda i: (i, 0))
    if carry:
        seq_spec = pl.BlockSpec((nseq, HIST, CONV_CH), lambda i: (0, 0, 0))
        n_state = nseq
    else:
        seq_spec = pl.BlockSpec((nseq, HIST, CONV_CH), lambda i: (i, 0, 0))
        n_state = t // seqlen
    consts = [g, wglu, wgate, bgate, wa, dww, dwb, lng, lnb, wc, wo, postg]
    return pl.pallas_call(
        functools.partial(_merge_kernel, nseq=nseq, seqlen=seqlen, carry=carry),
        out_shape=[jax.ShapeDtypeStruct((t, d), F32),
                   jax.ShapeDtypeStruct((n_state, HIST, CONV_CH), F32)],
        grid=(t // tm,),
        in_specs=[row(d), row(ATTN_DIM), seq_spec] + [_const_spec(c.shape) for c in consts],
        out_specs=[row(d), seq_spec],
        scratch_shapes=[pltpu.VMEM((nseq, HIST_ROW0 + HIST + seqlen, CONV_CH), F32)],
        compiler_params=pltpu.CompilerParams(dimension_semantics=("arbitrary",),
                                             vmem_limit_bytes=VMEM_LIMIT),
        name="branch_merge",
    )(x, a, hist, *consts)


def kernel(x_prompt, x_sample, cache_k, cache_v, cache_idx_k, state_conv,
           ffn1_pre_g, ffn1_w_gate, ffn1_w_up, ffn1_w_down, ffn1_post_g,
           mix_pre_g, w_in, b_gate, w_attn_o, conv_dw_w, conv_dw_b, conv_ln_g, conv_ln_b,
           w_conv_o, w_out, mix_post_g,
           ffn2_pre_g, ffn2_w_gate, ffn2_w_up, ffn2_w_down, ffn2_post_g):
    bp, lp, d = x_prompt.shape
    db, ts, _ = x_sample.shape
    depth, _, past = cache_k.shape[:3]
    assert bp == 1 and depth == 1
    topk_p = min(TOPK_MAX, lp // 4)
    topk_s = min(TOPK_MAX, (past + ts) // 4)

    vec = lambda a: a.reshape(1, -1)
    bf = lambda a: a.astype(BF16)
    ffn1 = (vec(ffn1_pre_g[0]), bf(ffn1_w_gate[0]), bf(ffn1_w_up[0]), bf(ffn1_w_down[0]), vec(ffn1_post_g[0]))
    ffn2 = (vec(ffn2_pre_g[0]), bf(ffn2_w_gate[0]), bf(ffn2_w_up[0]), bf(ffn2_w_down[0]), vec(ffn2_post_g[0]))

    widths = (IDX_HEADS * IDX_DIM, IDX_HEADS, IDX_DIM, ATTN_DIM, ATTN_DIM, ATTN_DIM, 2 * CONV_CH, 2 * d)
    offs = [0]
    for wd_ in widths:
        offs.append(offs[-1] + wd_)
    wi = w_in[0]
    col = lambda k: wi[:, offs[k]:offs[k + 1]]
    w_iw = jnp.pad(col(1), ((0, 0), (0, LANES - IDX_HEADS)))
    w2 = bf(jnp.concatenate([col(0), col(3), col(4), col(5), col(2), col(2), w_iw], axis=1))
    merge_w = (vec(mix_pre_g[0]), bf(col(6)), bf(col(7)), vec(b_gate[0]), bf(w_attn_o[0]),
               conv_dw_w[0], vec(conv_dw_b[0]), vec(conv_ln_g[0]), vec(conv_ln_b[0]),
               bf(w_conv_o[0]), bf(w_out[0]), vec(mix_post_g[0]))

    pos_p = jnp.arange(lp, dtype=I32)
    pos_s = jnp.tile(past + jnp.arange(ts, dtype=I32), db)

    hp = _ffn_half(x_prompt.reshape(lp, d), *ffn1)
    iq, q, kf, kb, vf, vb, ikf, ik2, iw = _mixer_proj(hp, vec(mix_pre_g[0]), w2, _rope_tables(pos_p))
    a = _prompt_attention(q, iq, iw, kb, vb, ik2, topk_p)
    hp, conv_p = _branch_merge(hp, a, jnp.zeros((1, HIST, CONV_CH), F32), *merge_w,
                               nseq=1, seqlen=256, carry=True)
    hp = _ffn_half(hp, *ffn2)
    outs_p = (kf.reshape(1, 1, lp, N_HEADS, HEAD_DIM), vf.reshape(1, 1, lp, N_HEADS, HEAD_DIM),
              ikf.reshape(1, 1, lp, IDX_DIM), conv_p.reshape(1, 1, HIST, CONV_CH))

    n_s = db * ts
    hs = _ffn_half(x_sample.reshape(n_s, d), *ffn1)
    iq, q, kf, kb, vf, vb, ikf, ik2, iw = _mixer_proj(hs, vec(mix_pre_g[0]), w2, _rope_tables(pos_s))
    a = _sample_attention(q, iq, iw, kb, vb, ik2, cache_k[0].reshape(db, past, ATTN_DIM),
                          cache_v[0].reshape(db, past, ATTN_DIM), cache_idx_k[0], ts, topk_s)
    hs, conv_s = _branch_merge(hs, a, state_conv[0], *merge_w, nseq=8, seqlen=ts, carry=False)
    hs = _ffn_half(hs, *ffn2)
    outs_s = (kf.reshape(1, db, ts, N_HEADS, HEAD_DIM), vf.reshape(1, db, ts, N_HEADS, HEAD_DIM),
              ikf.reshape(1, db, ts, IDX_DIM), conv_s.reshape(1, db, HIST, CONV_CH))

    return (hp.reshape(1, lp, d), hs.reshape(db, ts, d)) + outs_p + outs_s
```

```python
import functools
import math

import jax
import jax.numpy as jnp
from jax import lax
from jax.experimental import pallas as pl
from jax.experimental.pallas import tpu as pltpu

F32 = jnp.float32
BF16 = jnp.bfloat16
I32 = jnp.int32

CHUNK = 64
N_HEADS = 8
HEAD_DIM = 64
ATTN_DIM = N_HEADS * HEAD_DIM
IDX_HEADS = 8
IDX_DIM = 64
IDX_W_SCALE = (IDX_HEADS * IDX_DIM) ** -0.5
TOPK_MAX = 256
ROPE_THETA = 500000.0
ROT = HEAD_DIM // 4
CONV_CH = 512
CONV_WIDTH = 31
HIST = CONV_WIDTH - 1
EPS = 1e-6

LANES = 128
VMEM_LIMIT = 56 * 1024 * 1024
INT_MIN = -(2 ** 31)
NEG_BIG = -1e30

_NT = (((1,), (1,)), ((), ()))


def _rms(x, g):
    return x * lax.rsqrt(jnp.mean(x * x, axis=-1, keepdims=True) + EPS) * g


def _sigmoid(x):
    return 1.0 / (1.0 + jnp.exp(-x))


def _const_spec(shape):
    nd = len(shape)
    return pl.BlockSpec(shape, lambda *_: (0,) * nd, pipeline_mode=pl.Buffered(1))


def _ffn_kernel(x_ref, pre_g_ref, wg_ref, wu_ref, wd_ref, post_g_ref, o_ref):
    x = x_ref[...]
    h = _rms(x, pre_g_ref[...]).astype(BF16)
    g = jnp.dot(h, wg_ref[...], preferred_element_type=F32)
    u = jnp.dot(h, wu_ref[...], preferred_element_type=F32)
    a = (g * _sigmoid(g) * u).astype(BF16)
    f = jnp.dot(a, wd_ref[...], preferred_element_type=F32)
    o_ref[...] = x + 0.5 * _rms(f, post_g_ref[...])


def _ffn_half(x, pre_g, wg, wu, wd, post_g, tm=256):
    t, d = x.shape
    dff = wg.shape[1]
    row = pl.BlockSpec((tm, d), lambda i: (i, 0))
    return pl.pallas_call(
        _ffn_kernel,
        out_shape=jax.ShapeDtypeStruct((t, d), F32),
        grid=(t // tm,),
        in_specs=[row, _const_spec((1, d)), _const_spec((d, dff)), _const_spec((d, dff)),
                  _const_spec((dff, d)), _const_spec((1, d))],
        out_specs=row,
        compiler_params=pltpu.CompilerParams(dimension_semantics=("parallel",),
                                             vmem_limit_bytes=VMEM_LIMIT),
        name="ffn_half",
    )(x, pre_g, wg, wu, wd, post_g)


def _rope_tables(pos):
    half = ROT // 2
    inv_freq = jnp.exp(-math.log(ROPE_THETA) * jnp.arange(half, dtype=F32) * (2.0 / ROT))
    ang = pos.astype(F32)[:, None] * inv_freq[None, :]
    cos, sin = jnp.cos(ang), jnp.sin(ang)
    t = pos.shape[0]
    ones = jnp.ones((t, HEAD_DIM - ROT), F32)
    c = jnp.concatenate([cos, cos, ones], axis=1)
    s_up = jnp.concatenate([-sin, jnp.zeros((t, HEAD_DIM - half), F32)], axis=1)
    s_dn = jnp.concatenate([jnp.zeros((t, half), F32), sin, jnp.zeros((t, HEAD_DIM - ROT), F32)], axis=1)
    dup = lambda a: jnp.concatenate([a, a], axis=1)
    return dup(c), dup(s_up), dup(s_dn)


def _proj_kernel(x_ref, g_ref, w_ref, c_ref, sup_ref, sdn_ref,
                 iq_o, q_o, kf_o, kb_o, vf_o, vb_o, ikf_o, ik2_o, iw_o):
    n = _rms(x_ref[...], g_ref[...]).astype(BF16)
    p = jnp.dot(n, w_ref[...], preferred_element_type=F32)
    c, s_up, s_dn = c_ref[...], sup_ref[...], sdn_ref[...]

    def rope(x):
        return x * c + pltpu.roll(x, LANES - ROT // 2, 1) * s_up + pltpu.roll(x, ROT // 2, 1) * s_dn

    a = ATTN_DIM
    for gidx in range(a // LANES):
        sl = slice(gidx * LANES, (gidx + 1) * LANES)
        iq_o[:, sl] = rope(p[:, sl]).astype(BF16)
        q_o[:, sl] = (rope(p[:, a + gidx * LANES:a + (gidx + 1) * LANES]) * (HEAD_DIM ** -0.5)).astype(BF16)
        k = rope(p[:, 2 * a + gidx * LANES:2 * a + (gidx + 1) * LANES])
        kf_o[:, sl] = k
        kb_o[:, sl] = k.astype(BF16)
        v = p[:, 3 * a + gidx * LANES:3 * a + (gidx + 1) * LANES]
        vf_o[:, sl] = v
        vb_o[:, sl] = v.astype(BF16)
    ik = rope(p[:, 4 * a:4 * a + LANES])
    ikf_o[...] = ik[:, :IDX_DIM]
    ik2_o[...] = ik.astype(BF16)
    iw_o[...] = p[:, 4 * a + LANES:4 * a + 2 * LANES] * IDX_W_SCALE


def _mixer_proj(x, g, w2, tables, tm=256):
    t, d = x.shape
    ncol = w2.shape[1]
    row = lambda width: pl.BlockSpec((tm, width), lambda i: (i, 0))
    a = ATTN_DIM
    outs = [
        jax.ShapeDtypeStruct((t, a), BF16),
        jax.ShapeDtypeStruct((t, a), BF16),
        jax.ShapeDtypeStruct((t, a), F32),
        jax.ShapeDtypeStruct((t, a), BF16),
        jax.ShapeDtypeStruct((t, a), F32),
        jax.ShapeDtypeStruct((t, a), BF16),
        jax.ShapeDtypeStruct((t, IDX_DIM), F32),
        jax.ShapeDtypeStruct((t, LANES), BF16),
        jax.ShapeDtypeStruct((t, LANES), F32),
    ]
    return pl.pallas_call(
        _proj_kernel,
        out_shape=outs,
        grid=(t // tm,),
        in_specs=[row(d), _const_spec((1, d)), _const_spec((d, ncol)), row(LANES), row(LANES), row(LANES)],
        out_specs=[row(a), row(a), row(a), row(a), row(a), row(a), row(IDX_DIM), row(LANES), row(LANES)],
        compiler_params=pltpu.CompilerParams(dimension_semantics=("parallel",),
                                             vmem_limit_bytes=VMEM_LIMIT),
        name="mixer_proj",
    )(x, g, w2, *tables)


def _sortable(x):
    b = lax.bitcast_convert_type(x, I32)
    k = b ^ ((b >> 31) & 0x7FFFFFFF)
    return jnp.where(x == 0.0, 0, k)


def _head_lhs(x, h):
    grp, odd = divmod(h, 2)
    xg = x[:, grp * LANES:(grp + 1) * LANES].astype(F32)
    lane = lax.broadcasted_iota(I32, xg.shape, 1)
    keep = (lane >= HEAD_DIM) if odd else (lane < HEAD_DIM)
    return jnp.where(keep, xg, 0.0).astype(BF16)


def _stack_heads(x_ref, dst_ref, rows):
    x = x_ref[...]
    for h in range(N_HEADS):
        dst_ref[h * rows:(h + 1) * rows, :] = _head_lhs(x, h)


def _spread_weights(w, wrep_ref):
    for h in range(IDX_HEADS):
        wrep_ref[h] = jnp.broadcast_to(w[:, h:h + 1], wrep_ref.shape[1:])


def _index_scores(iqs_ref, wrep_ref, ikc):
    rows = wrep_ref.shape[1]
    d = lax.dot_general(iqs_ref[...], ikc, _NT, preferred_element_type=F32)
    groups = []
    for l in range(ikc.shape[0] // LANES):
        acc = None
        for h in range(IDX_HEADS):
            term = jnp.maximum(d[h * rows:(h + 1) * rows, l * LANES:(l + 1) * LANES], 0.0) * wrep_ref[h]
            acc = term if acc is None else acc + term
        groups.append(acc)
    return jnp.concatenate(groups, axis=1)


def _lane_fold(x):
    out = x[:, :LANES]
    for l in range(1, x.shape[1] // LANES):
        out = out + x[:, l * LANES:(l + 1) * LANES]
    return out


def _select_threshold(count_ge, n_adm, topk):
    def cond(st):
        i, _, cnt = st
        return jnp.logical_and(i < 32, jnp.max(cnt) > topk)

    def step(st):
        i, t, cnt = st
        cand = t + jnp.left_shift(jnp.int32(1), 31 - i)
        c = count_ge(cand)
        ok = c >= topk
        return i + 1, jnp.where(ok, cand, t), jnp.where(ok, c, cnt)

    t0 = jnp.full(n_adm.shape, INT_MIN, I32)
    _, t, cnt = lax.while_loop(cond, step, (jnp.int32(0), t0, n_adm))
    return jnp.maximum(t, INT_MIN + 1), cnt


def _bias_plain(key, t):
    return jnp.where(key >= t, 0.0, NEG_BIG)


def _bias_ties(key, t, need, run):
    sc = key.shape[1]
    eq = key == t
    eqf = jnp.where(eq, 1.0, 0.0)
    r = lax.broadcasted_iota(I32, (sc, sc), 0)
    c = lax.broadcasted_iota(I32, (sc, sc), 1)
    tri = jnp.where(r < c, 1.0, 0.0).astype(BF16)
    before = run + jnp.dot(eqf.astype(BF16), tri, preferred_element_type=F32)
    sel = (key > t) | (eq & (before < need))
    return jnp.where(sel, 0.0, NEG_BIG), run + jnp.sum(eqf, axis=1, keepdims=True)


def _attend_chunk(qs_ref, bias, kc, vc, m_ref, l_ref, acc_ref):
    pairs = N_HEADS // 2
    prow = qs_ref.shape[0] // pairs
    nrep = kc.shape[0] // LANES
    bias2 = jnp.concatenate([bias, bias], axis=0)
    scores = [lax.dot_general(qs_ref[g * prow:(g + 1) * prow, :], kc[:, g * LANES:(g + 1) * LANES], _NT,
                              preferred_element_type=F32) + bias2 for g in range(pairs)]
    probs, alphas = [], []
    for g in range(pairs):
        s = scores[g]
        m_prev = m_ref[g]
        m_next = jnp.maximum(m_prev, jnp.max(s, axis=1, keepdims=True))
        alpha = jnp.exp(m_prev - m_next)
        p = jnp.exp(s - jnp.concatenate([m_next] * nrep, axis=1))
        l_ref[g] = alpha * l_ref[g] + _lane_fold(p)
        m_ref[g] = m_next
        probs.append(p.astype(BF16))
        alphas.append(alpha)
    for g in range(pairs):
        pv = jnp.dot(probs[g], vc[:, g * LANES:(g + 1) * LANES], preferred_element_type=F32)
        acc_ref[g] = alphas[g] * acc_ref[g] + pv


def _init_attend(m_ref, l_ref, acc_ref):
    m_ref[...] = jnp.full(m_ref.shape, NEG_BIG, F32)
    l_ref[...] = jnp.zeros(l_ref.shape, F32)
    acc_ref[...] = jnp.zeros(acc_ref.shape, F32)


def _finish_attend(o_ref, l_ref, acc_ref):
    rows = acc_ref.shape[1] // 2
    lane = lax.broadcasted_iota(I32, (rows, LANES), 1)
    for g in range(N_HEADS // 2):
        o = acc_ref[g] / jnp.sum(l_ref[g], axis=1, keepdims=True)
        o_ref[:, g * LANES:(g + 1) * LANES] = jnp.where(lane < HEAD_DIM, o[:rows], o[rows:]).astype(o_ref.dtype)


def _prompt_attn_kernel(q_ref, iq_ref, iw_ref, kb_ref, vb_ref, ik2_ref, o_ref,
                        key_ref, iqs_ref, qs_ref, wrep_ref, m_ref, l_ref, acc_ref, *, tq, sc, topk):
    j = pl.program_id(0)
    n_chunks = ((j + 1) * tq + sc - 1) // sc

    _stack_heads(iq_ref, iqs_ref, tq)
    _stack_heads(q_ref, qs_ref, tq)
    _spread_weights(iw_ref[...], wrep_ref)
    row = lax.broadcasted_iota(I32, (tq, 1), 0)
    limit = j * tq + (row // CHUNK + 1) * CHUNK

    def score_chunk(c, carry):
        off = pl.multiple_of(c * sc, sc)
        acc = _index_scores(iqs_ref, wrep_ref, ik2_ref[pl.ds(off, sc), :])
        sidx = off + lax.broadcasted_iota(I32, (tq, sc), 1)
        key_ref[c] = jnp.where(sidx < limit, _sortable(acc), INT_MIN)
        return carry
    lax.fori_loop(0, n_chunks, score_chunk, 0)

    def count_ge(cand):
        def body(c, acc):
            return acc + _lane_fold(jnp.where(key_ref[c] >= cand, 1.0, 0.0))
        part = lax.fori_loop(0, n_chunks, body, jnp.zeros((tq, LANES), F32))
        return jnp.sum(part, axis=1, keepdims=True)

    t, n_ge = _select_threshold(count_ge, limit.astype(F32), topk)
    has_ties = jnp.max(n_ge) > topk

    @pl.when(jnp.logical_not(has_ties))
    def _():
        def body(c, carry):
            key_ref[c] = lax.bitcast_convert_type(_bias_plain(key_ref[c], t), I32)
            return carry
        lax.fori_loop(0, n_chunks, body, 0)

    @pl.when(has_ties)
    def _():
        need = topk - count_ge(t + 1)
        def body(c, run):
            bias, run = _bias_ties(key_ref[c], t, need, run)
            key_ref[c] = lax.bitcast_convert_type(bias, I32)
            return run
        lax.fori_loop(0, n_chunks, body, jnp.zeros((tq, 1), F32))

    _init_attend(m_ref, l_ref, acc_ref)

    def attend(c, carry):
        off = pl.multiple_of(c * sc, sc)
        bias = lax.bitcast_convert_type(key_ref[c], F32)
        _attend_chunk(qs_ref, bias, kb_ref[pl.ds(off, sc), :], vb_ref[pl.ds(off, sc), :],
                      m_ref, l_ref, acc_ref)
        return carry
    lax.fori_loop(0, n_chunks, attend, 0)
    _finish_attend(o_ref, l_ref, acc_ref)


def _attn_scratch(rows):
    pairs = N_HEADS // 2
    return [
        pltpu.VMEM((N_HEADS * rows, LANES), BF16),
        pltpu.VMEM((N_HEADS * rows, LANES), BF16),
        pltpu.VMEM((IDX_HEADS, rows, LANES), F32),
        pltpu.VMEM((pairs, 2 * rows, LANES), F32),
        pltpu.VMEM((pairs, 2 * rows, LANES), F32),
        pltpu.VMEM((pairs, 2 * rows, LANES), F32),
    ]


def _prompt_attention(q, iq, iw, kb, vb, ik2, topk, tq=128, sc=512):
    s = kb.shape[0]
    row = lambda width: pl.BlockSpec((tq, width), lambda j: (j, 0))
    whole = pl.BlockSpec(memory_space=pltpu.VMEM)
    return pl.pallas_call(
        functools.partial(_prompt_attn_kernel, tq=tq, sc=sc, topk=topk),
        out_shape=jax.ShapeDtypeStruct((s, ATTN_DIM), BF16),
        grid=(s // tq,),
        in_specs=[row(ATTN_DIM), row(ATTN_DIM), row(LANES), whole, whole, whole],
        out_specs=row(ATTN_DIM),
        scratch_shapes=[pltpu.VMEM((s // sc, tq, sc), I32)] + _attn_scratch(tq),
        compiler_params=pltpu.CompilerParams(dimension_semantics=("arbitrary",),
                                             vmem_limit_bytes=VMEM_LIMIT),
        name="prompt_attention",
    )(q, iq, iw, kb, vb, ik2)


def _sample_attn_kernel(q_ref, iq_ref, iw_ref, kn_ref, vn_ref, ikn_ref, ck_ref, cv_ref, cik_ref, o_ref,
                        key_ref, keyn_ref, iqs_ref, qs_ref, wrep_ref, m_ref, l_ref, acc_ref, *, sc, topk):
    ts = q_ref.shape[0]
    past = ck_ref.shape[1]
    n_chunks = past // sc

    _stack_heads(iq_ref, iqs_ref, ts)
    _stack_heads(q_ref, qs_ref, ts)
    _spread_weights(iw_ref[...], wrep_ref)

    pad = lambda x: jnp.concatenate([x, jnp.zeros((LANES - ts, x.shape[1]), x.dtype)], axis=0)
    new_ok = lax.broadcasted_iota(I32, (ts, LANES), 1) < ts

    def cache_ik(c):
        ik = cik_ref[0, c * sc:(c + 1) * sc, :].astype(BF16)
        return jnp.concatenate([ik, ik], axis=1)

    for c in range(n_chunks):
        key_ref[c] = _sortable(_index_scores(iqs_ref, wrep_ref,cache_ik(c)))
    keyn_ref[...] = jnp.where(new_ok, _sortable(_index_scores(iqs_ref, wrep_ref,pad(ikn_ref[...]))), INT_MIN)

    def count_ge(cand):
        part = jnp.where(keyn_ref[...] >= cand, 1.0, 0.0)
        for c in range(n_chunks):
            part = part + _lane_fold(jnp.where(key_ref[c] >= cand, 1.0, 0.0))
        return jnp.sum(part, axis=1, keepdims=True)

    t, n_ge = _select_threshold(count_ge, jnp.full((ts, 1), past + ts, F32), topk)
    has_ties = jnp.max(n_ge) > topk

    @pl.when(jnp.logical_not(has_ties))
    def _():
        for c in range(n_chunks):
            key_ref[c] = lax.bitcast_convert_type(_bias_plain(key_ref[c], t), I32)
        keyn_ref[...] = lax.bitcast_convert_type(_bias_plain(keyn_ref[...], t), I32)

    @pl.when(has_ties)
    def _():
        need = topk - count_ge(t + 1)
        run = jnp.zeros((ts, 1), F32)
        for c in range(n_chunks):
            bias, run = _bias_ties(key_ref[c], t, need, run)
            key_ref[c] = lax.bitcast_convert_type(bias, I32)
        bias, run = _bias_ties(keyn_ref[...], t, need, run)
        keyn_ref[...] = lax.bitcast_convert_type(bias, I32)

    _init_attend(m_ref, l_ref, acc_ref)
    for c in range(n_chunks):
        _attend_chunk(qs_ref,lax.bitcast_convert_type(key_ref[c], F32),
                      ck_ref[0, c * sc:(c + 1) * sc, :].astype(BF16),
                      cv_ref[0, c * sc:(c + 1) * sc, :].astype(BF16), m_ref, l_ref, acc_ref)
    _attend_chunk(qs_ref,lax.bitcast_convert_type(keyn_ref[...], F32),
                  pad(kn_ref[...]), pad(vn_ref[...]), m_ref, l_ref, acc_ref)
    _finish_attend(o_ref, l_ref, acc_ref)


def _sample_attention(q, iq, iw, kb, vb, ik2, cache_k, cache_v, cache_ik, ts, topk, sc=512):
    t = q.shape[0]
    nb, past = cache_k.shape[:2]
    row = lambda width: pl.BlockSpec((ts, width), lambda b: (b, 0))
    cache = lambda width: pl.BlockSpec((1, past, width), lambda b: (b, 0, 0))
    return pl.pallas_call(
        functools.partial(_sample_attn_kernel, sc=sc, topk=topk),
        out_shape=jax.ShapeDtypeStruct((t, ATTN_DIM), BF16),
        grid=(nb,),
        in_specs=[row(ATTN_DIM), row(ATTN_DIM), row(LANES), row(ATTN_DIM), row(ATTN_DIM), row(LANES),
                  cache(ATTN_DIM), cache(ATTN_DIM), cache(IDX_DIM)],
        out_specs=row(ATTN_DIM),
        scratch_shapes=[pltpu.VMEM((past // sc, ts, sc), I32), pltpu.VMEM((ts, LANES), I32)] + _attn_scratch(ts),
        compiler_params=pltpu.CompilerParams(dimension_semantics=("parallel",),
                                             vmem_limit_bytes=VMEM_LIMIT),
        name="sample_attention",
    )(q, iq, iw, kb, vb, ik2, cache_k, cache_v, cache_ik)


HIST_ROW0 = 2


def _merge_kernel(x_ref, a_ref, hist_ref, g_ref, wglu_ref, wgate_ref, bgate_ref, wa_ref,
                  dww_ref, dwb_ref, lng_ref, lnb_ref, wc_ref, wo_ref, postg_ref,
                  y_ref, conv_o_ref, uext_ref, *, nseq, seqlen, carry):
    i = pl.program_id(0)
    x = x_ref[...]
    n = _rms(x, g_ref[...]).astype(BF16)
    glu = jnp.dot(n, wglu_ref[...], preferred_element_type=F32)
    u = glu[:, :CONV_CH] * _sigmoid(glu[:, CONV_CH:])

    new0 = HIST_ROW0 + HIST
    if carry:
        @pl.when(i == 0)
        def _():
            uext_ref[:, HIST_ROW0:new0, :] = hist_ref[...]

        @pl.when(i > 0)
        def _():
            uext_ref[:, HIST_ROW0:new0, :] = uext_ref[:, seqlen + HIST_ROW0:seqlen + new0, :]
    else:
        uext_ref[:, HIST_ROW0:new0, :] = hist_ref[...]
    uext_ref[:, new0:new0 + seqlen, :] = u.reshape(nseq, seqlen, CONV_CH)
    conv_o_ref[...] = uext_ref[:, seqlen + HIST_ROW0:seqlen + new0, :]

    dww = dww_ref[...]
    cdw = None
    for tap in range(CONV_WIDTH):
        term = uext_ref[:, HIST_ROW0 + tap:HIST_ROW0 + tap + seqlen, :] * dww[tap:tap + 1, :]
        cdw = term if cdw is None else cdw + term
    cdw = cdw.reshape(nseq * seqlen, CONV_CH) + dwb_ref[...]

    mu = jnp.mean(cdw, axis=-1, keepdims=True)
    xc = cdw - mu
    var = jnp.mean(xc * xc, axis=-1, keepdims=True)
    ln = xc * lax.rsqrt(var + EPS) * lng_ref[...] + lnb_ref[...]
    cact = (ln * _sigmoid(ln)).astype(BF16)
    c_proj = jnp.dot(cact, wc_ref[...], preferred_element_type=F32)
    a_proj = jnp.dot(a_ref[...], wa_ref[...], preferred_element_type=F32)

    d = x.shape[1]
    gate = _sigmoid(jnp.dot(n, wgate_ref[...], preferred_element_type=F32) + bgate_ref[...])
    merged = (gate[:, :d] * a_proj + gate[:, d:] * c_proj).astype(BF16)
    y = jnp.dot(merged, wo_ref[...], preferred_element_type=F32)
    y_ref[...] = x + _rms(y, postg_ref[...])


def _branch_merge(x, a, hist, g, wglu, wgate, bgate, wa, dww, dwb, lng, lnb, wc, wo, postg,
                  *, nseq, seqlen, carry):
    t, d = x.shape
    tm = nseq * seqlen
    row = lambda width: pl.BlockSpec((tm, width), lambda i: (i, 0))
    if carry:
        seq_spec = pl.BlockSpec((nseq, HIST, CONV_CH), lambda i: (0, 0, 0))
        n_state = nseq
    else:
        seq_spec = pl.BlockSpec((nseq, HIST, CONV_CH), lambda i: (i, 0, 0))
        n_state = t // seqlen
    consts = [g, wglu, wgate, bgate, wa, dww, dwb, lng, lnb, wc, wo, postg]
    return pl.pallas_call(
        functools.partial(_merge_kernel, nseq=nseq, seqlen=seqlen, carry=carry),
        out_shape=[jax.ShapeDtypeStruct((t, d), F32),
                   jax.ShapeDtypeStruct((n_state, HIST, CONV_CH), F32)],
        grid=(t // tm,),
        in_specs=[row(d), row(ATTN_DIM), seq_spec] + [_const_spec(c.shape) for c in consts],
        out_specs=[row(d), seq_spec],
        scratch_shapes=[pltpu.VMEM((nseq, HIST_ROW0 + HIST + seqlen, CONV_CH), F32)],
        compiler_params=pltpu.CompilerParams(dimension_semantics=("arbitrary",),
                                             vmem_limit_bytes=VMEM_LIMIT),
        name="branch_merge",
    )(x, a, hist, *consts)


def kernel(x_prompt, x_sample, cache_k, cache_v, cache_idx_k, state_conv,
           ffn1_pre_g, ffn1_w_gate, ffn1_w_up, ffn1_w_down, ffn1_post_g,
           mix_pre_g, w_in, b_gate, w_attn_o, conv_dw_w, conv_dw_b, conv_ln_g, conv_ln_b,
           w_conv_o, w_out, mix_post_g,
           ffn2_pre_g, ffn2_w_gate, ffn2_w_up, ffn2_w_down, ffn2_post_g):
    bp, lp, d = x_prompt.shape
    db, ts, _ = x_sample.shape
    depth, _, past = cache_k.shape[:3]
    assert bp == 1 and depth == 1
    topk_p = min(TOPK_MAX, lp // 4)
    topk_s = min(TOPK_MAX, (past + ts) // 4)

    vec = lambda a: a.reshape(1, -1)
    bf = lambda a: a.astype(BF16)
    ffn1 = (vec(ffn1_pre_g[0]), bf(ffn1_w_gate[0]), bf(ffn1_w_up[0]), bf(ffn1_w_down[0]), vec(ffn1_post_g[0]))
    ffn2 = (vec(ffn2_pre_g[0]), bf(ffn2_w_gate[0]), bf(ffn2_w_up[0]), bf(ffn2_w_down[0]), vec(ffn2_post_g[0]))

    widths = (IDX_HEADS * IDX_DIM, IDX_HEADS, IDX_DIM, ATTN_DIM, ATTN_DIM, ATTN_DIM, 2 * CONV_CH, 2 * d)
    offs = [0]
    for wd_ in widths:
        offs.append(offs[-1] + wd_)
    wi = w_in[0]
    col = lambda k: wi[:, offs[k]:offs[k + 1]]
    w_iw = jnp.pad(col(1), ((0, 0), (0, LANES - IDX_HEADS)))
    w2 = bf(jnp.concatenate([col(0), col(3), col(4), col(5), col(2), col(2), w_iw], axis=1))
    merge_w = (vec(mix_pre_g[0]), bf(col(6)), bf(col(7)), vec(b_gate[0]), bf(w_attn_o[0]),
               conv_dw_w[0], vec(conv_dw_b[0]), vec(conv_ln_g[0]), vec(conv_ln_b[0]),
               bf(w_conv_o[0]), bf(w_out[0]), vec(mix_post_g[0]))

    pos_p = jnp.arange(lp, dtype=I32)
    pos_s = jnp.tile(past + jnp.arange(ts, dtype=I32), db)

    hp = _ffn_half(x_prompt.reshape(lp, d), *ffn1)
    iq, q, kf, kb, vf, vb, ikf, ik2, iw = _mixer_proj(hp, vec(mix_pre_g[0]), w2, _rope_tables(pos_p))
    a = _prompt_attention(q, iq, iw, kb, vb, ik2, topk_p)
    hp, conv_p = _branch_merge(hp, a, jnp.zeros((1, HIST, CONV_CH), F32), *merge_w,
                               nseq=1, seqlen=256, carry=True)
    hp = _ffn_half(hp, *ffn2)
    outs_p = (kf.reshape(1, 1, lp, N_HEADS, HEAD_DIM), vf.reshape(1, 1, lp, N_HEADS, HEAD_DIM),
              ikf.reshape(1, 1, lp, IDX_DIM), conv_p.reshape(1, 1, HIST, CONV_CH))

    n_s = db * ts
    hs = _ffn_half(x_sample.reshape(n_s, d), *ffn1)
    iq, q, kf, kb, vf, vb, ikf, ik2, iw = _mixer_proj(hs, vec(mix_pre_g[0]), w2, _rope_tables(pos_s))
    a = _sample_attention(q, iq, iw, kb, vb, ik2, cache_k[0].reshape(db, past, ATTN_DIM),
                          cache_v[0].reshape(db, past, ATTN_DIM), cache_idx_k[0], ts, topk_s)
    hs, conv_s = _branch_merge(hs, a, state_conv[0], *merge_w, nseq=8, seqlen=ts, carry=False)
    hs = _ffn_half(hs, *ffn2)
    outs_s = (kf.reshape(1, db, ts, N_HEADS, HEAD_DIM), vf.reshape(1, db, ts, N_HEADS, HEAD_DIM),
              ikf.reshape(1, db, ts, IDX_DIM), conv_s.reshape(1, db, HIST, CONV_CH))

    return (hp.reshape(1, lp, d), hs.reshape(db, ts, d)) + outs_p + outs_s
```

```python
import functools
import math

import jax
import jax.numpy as jnp
from jax import lax
from jax.experimental import pallas as pl
from jax.experimental.pallas import tpu as pltpu

F32 = jnp.float32
BF16 = jnp.bfloat16
I32 = jnp.int32

CHUNK = 64
N_HEADS = 8
HEAD_DIM = 64
ATTN_DIM = N_HEADS * HEAD_DIM
IDX_HEADS = 8
IDX_DIM = 64
IDX_W_SCALE = (IDX_HEADS * IDX_DIM) ** -0.5
TOPK_MAX = 256
ROPE_THETA = 500000.0
ROT = HEAD_DIM // 4
CONV_CH = 512
CONV_WIDTH = 31
HIST = CONV_WIDTH - 1
EPS = 1e-6

LANES = 128
VMEM_LIMIT = 56 * 1024 * 1024
INT_MIN = -(2 ** 31)
NEG_BIG = -1e30
Q_SCALE = HEAD_DIM ** -0.5 * math.log2(math.e)

_NT = (((1,), (1,)), ((), ()))


def _rms(x, g):
    return x * lax.rsqrt(jnp.mean(x * x, axis=-1, keepdims=True) + EPS) * g


def _sigmoid(x):
    return 1.0 / (1.0 + jnp.exp(-x))


def _const_spec(shape):
    nd = len(shape)
    return pl.BlockSpec(shape, lambda *_: (0,) * nd, pipeline_mode=pl.Buffered(1))


def _ffn_kernel(x_ref, pre_g_ref, wg_ref, wu_ref, wd_ref, post_g_ref, o_ref):
    x = x_ref[...]
    h = _rms(x, pre_g_ref[...]).astype(BF16)
    g = jnp.dot(h, wg_ref[...], preferred_element_type=F32)
    u = jnp.dot(h, wu_ref[...], preferred_element_type=F32)
    a = (g * _sigmoid(g) * u).astype(BF16)
    f = jnp.dot(a, wd_ref[...], preferred_element_type=F32)
    o_ref[...] = x + 0.5 * _rms(f, post_g_ref[...])


def _ffn_half(x, pre_g, wg, wu, wd, post_g, tm=256):
    t, d = x.shape
    dff = wg.shape[1]
    row = pl.BlockSpec((tm, d), lambda i: (i, 0))
    return pl.pallas_call(
        _ffn_kernel,
        out_shape=jax.ShapeDtypeStruct((t, d), F32),
        grid=(t // tm,),
        in_specs=[row, _const_spec((1, d)), _const_spec((d, dff)), _const_spec((d, dff)),
                  _const_spec((dff, d)), _const_spec((1, d))],
        out_specs=row,
        compiler_params=pltpu.CompilerParams(dimension_semantics=("parallel",),
                                             vmem_limit_bytes=VMEM_LIMIT),
        name="ffn_half",
    )(x, pre_g, wg, wu, wd, post_g)


def _rope_tables(pos):
    half = ROT // 2
    inv_freq = jnp.exp(-math.log(ROPE_THETA) * jnp.arange(half, dtype=F32) * (2.0 / ROT))
    ang = pos.astype(F32)[:, None] * inv_freq[None, :]
    cos, sin = jnp.cos(ang), jnp.sin(ang)
    t = pos.shape[0]
    ones = jnp.ones((t, HEAD_DIM - ROT), F32)
    c = jnp.concatenate([cos, cos, ones], axis=1)
    s_up = jnp.concatenate([-sin, jnp.zeros((t, HEAD_DIM - half), F32)], axis=1)
    s_dn = jnp.concatenate([jnp.zeros((t, half), F32), sin, jnp.zeros((t, HEAD_DIM - ROT), F32)], axis=1)
    dup = lambda a: jnp.concatenate([a, a], axis=1)
    return dup(c), dup(s_up), dup(s_dn)


def _proj_kernel(x_ref, g_ref, w_ref, c_ref, sup_ref, sdn_ref,
                 iq_o, q_o, kf_o, kb_o, vf_o, vb_o, ikf_o, ik2_o, iw_o):
    n = _rms(x_ref[...], g_ref[...]).astype(BF16)
    p = jnp.dot(n, w_ref[...], preferred_element_type=F32)
    c, s_up, s_dn = c_ref[...], sup_ref[...], sdn_ref[...]

    def rope(x):
        return x * c + pltpu.roll(x, LANES - ROT // 2, 1) * s_up + pltpu.roll(x, ROT // 2, 1) * s_dn

    a = ATTN_DIM
    for gidx in range(a // LANES):
        sl = slice(gidx * LANES, (gidx + 1) * LANES)
        iq_o[:, sl] = rope(p[:, sl]).astype(BF16)
        q_o[:, sl] = (rope(p[:, a + gidx * LANES:a + (gidx + 1) * LANES]) * Q_SCALE).astype(BF16)
        k = rope(p[:, 2 * a + gidx * LANES:2 * a + (gidx + 1) * LANES])
        kf_o[:, sl] = k
        kb_o[:, sl] = k.astype(BF16)
        v = p[:, 3 * a + gidx * LANES:3 * a + (gidx + 1) * LANES]
        vf_o[:, sl] = v
        vb_o[:, sl] = v.astype(BF16)
    ik = rope(p[:, 4 * a:4 * a + LANES])
    ikf_o[...] = ik[:, :IDX_DIM]
    ik2_o[...] = ik.astype(BF16)
    iw_o[...] = p[:, 4 * a + LANES:4 * a + 2 * LANES] * IDX_W_SCALE


def _mixer_proj(x, g, w2, tables, tm=256):
    t, d = x.shape
    ncol = w2.shape[1]
    row = lambda width: pl.BlockSpec((tm, width), lambda i: (i, 0))
    a = ATTN_DIM
    outs = [
        jax.ShapeDtypeStruct((t, a), BF16),
        jax.ShapeDtypeStruct((t, a), BF16),
        jax.ShapeDtypeStruct((t, a), F32),
        jax.ShapeDtypeStruct((t, a), BF16),
        jax.ShapeDtypeStruct((t, a), F32),
        jax.ShapeDtypeStruct((t, a), BF16),
        jax.ShapeDtypeStruct((t, IDX_DIM), F32),
        jax.ShapeDtypeStruct((t, LANES), BF16),
        jax.ShapeDtypeStruct((t, LANES), F32),
    ]
    return pl.pallas_call(
        _proj_kernel,
        out_shape=outs,
        grid=(t // tm,),
        in_specs=[row(d), _const_spec((1, d)), _const_spec((d, ncol)), row(LANES), row(LANES), row(LANES)],
        out_specs=[row(a), row(a), row(a), row(a), row(a), row(a), row(IDX_DIM), row(LANES), row(LANES)],
        compiler_params=pltpu.CompilerParams(dimension_semantics=("parallel",),
                                             vmem_limit_bytes=VMEM_LIMIT),
        name="mixer_proj",
    )(x, g, w2, *tables)


def _sortable(x):
    b = lax.bitcast_convert_type(x, I32)
    k = b ^ ((b >> 31) & 0x7FFFFFFF)
    return jnp.where(x == 0.0, 0, k)


def _head_lhs(x, h):
    grp, odd = divmod(h, 2)
    xg = x[:, grp * LANES:(grp + 1) * LANES].astype(F32)
    lane = lax.broadcasted_iota(I32, xg.shape, 1)
    keep = (lane >= HEAD_DIM) if odd else (lane < HEAD_DIM)
    return jnp.where(keep, xg, 0.0).astype(BF16)


def _stack_heads(x_ref, dst_ref, rows):
    x = x_ref[...]
    for h in range(N_HEADS):
        dst_ref[h * rows:(h + 1) * rows, :] = _head_lhs(x, h)


def _spread_weights(w, wrep_ref):
    for h in range(IDX_HEADS):
        wrep_ref[h] = jnp.broadcast_to(w[:, h:h + 1], wrep_ref.shape[1:])


def _index_scores(iqs_ref, wrep_ref, ikc):
    rows = wrep_ref.shape[1]
    d = lax.dot_general(iqs_ref[...], ikc, _NT, preferred_element_type=F32)
    groups = []
    for l in range(ikc.shape[0] // LANES):
        acc = None
        for h in range(IDX_HEADS):
            term = jnp.maximum(d[h * rows:(h + 1) * rows, l * LANES:(l + 1) * LANES], 0.0) * wrep_ref[h]
            acc = term if acc is None else acc + term
        groups.append(acc)
    return jnp.concatenate(groups, axis=1)


def _lane_fold(x):
    out = x[:, :LANES]
    for l in range(1, x.shape[1] // LANES):
        out = out + x[:, l * LANES:(l + 1) * LANES]
    return out


def _select_threshold(count_ge, n_adm, topk):
    c0 = count_ge(jnp.zeros(n_adm.shape, I32))
    c1 = count_ge(jnp.ones(n_adm.shape, I32))
    nonneg = c0 >= topk
    zero_tie = jnp.logical_and(nonneg, c1 < topk)

    def cond(st):
        i, _, cnt = st
        return jnp.logical_and(i < 32, jnp.max(jnp.where(zero_tie, 0.0, cnt)) > topk)

    def step(st):
        i, t, cnt = st
        cand = t + jnp.left_shift(jnp.int32(1), 31 - i)
        c = count_ge(cand)
        ok = c >= topk
        return i + 1, jnp.where(ok, cand, t), jnp.where(ok, c, cnt)

    t0 = jnp.where(nonneg, 0, INT_MIN)
    _, t, cnt = lax.while_loop(cond, step, (jnp.int32(1), t0, jnp.where(nonneg, c0, n_adm)))
    return jnp.maximum(t, INT_MIN + 1), cnt


def _bias_plain(key, t):
    return jnp.where(key >= t, 0.0, NEG_BIG)


def _bias_ties(key, t, need, run):
    sc = key.shape[1]
    eq = key == t
    eqf = jnp.where(eq, 1.0, 0.0)
    r = lax.broadcasted_iota(I32, (sc, sc), 0)
    c = lax.broadcasted_iota(I32, (sc, sc), 1)
    tri = jnp.where(r < c, 1.0, 0.0).astype(BF16)
    before = run + jnp.dot(eqf.astype(BF16), tri, preferred_element_type=F32)
    sel = (key > t) | (eq & (before < need))
    return jnp.where(sel, 0.0, NEG_BIG), run + jnp.sum(eqf, axis=1, keepdims=True)


def _attend_chunk(qs_ref, bias, kc, vc, m_ref, acc_ref):
    pairs = N_HEADS // 2
    prow = qs_ref.shape[0] // pairs
    nrep = kc.shape[0] // LANES
    bias2 = jnp.concatenate([bias, bias], axis=0)
    ones = jnp.ones((vc.shape[0], LANES), BF16)
    scores = [lax.dot_general(qs_ref[g * prow:(g + 1) * prow, :], kc[:, g * LANES:(g + 1) * LANES], _NT,
                              preferred_element_type=F32) + bias2 for g in range(pairs)]
    probs, alphas = [], []
    for g in range(pairs):
        s = scores[g]
        m_prev = m_ref[g]
        m_next = jnp.maximum(m_prev, jnp.max(s, axis=1, keepdims=True))
        alphas.append(jnp.exp2(m_prev - m_next))
        probs.append(jnp.exp2(s - jnp.concatenate([m_next] * nrep, axis=1)).astype(BF16))
        m_ref[g] = m_next
    for g in range(pairs):
        vext = jnp.concatenate([vc[:, g * LANES:(g + 1) * LANES], ones], axis=1)
        pv = jnp.dot(probs[g], vext, preferred_element_type=F32)
        acc_ref[g] = jnp.concatenate([alphas[g], alphas[g]], axis=1) * acc_ref[g] + pv


def _init_attend(m_ref, acc_ref):
    m_ref[...] = jnp.full(m_ref.shape, NEG_BIG, F32)
    acc_ref[...] = jnp.zeros(acc_ref.shape, F32)


def _finish_attend(o_ref, acc_ref):
    rows = acc_ref.shape[1] // 2
    lane = lax.broadcasted_iota(I32, (rows, LANES), 1)
    for g in range(N_HEADS // 2):
        acc = acc_ref[g]
        o = acc[:, :LANES] / acc[:, LANES:]
        o_ref[:, g * LANES:(g + 1) * LANES] = jnp.where(lane < HEAD_DIM, o[:rows], o[rows:]).astype(o_ref.dtype)


def _prompt_attn_kernel(q_ref, iq_ref, iw_ref, kb_ref, vb_ref, ik2_ref, o_ref,
                        key_ref, iqs_ref, qs_ref, wrep_ref, m_ref, acc_ref, *, tq, sc, topk):
    j = pl.program_id(0)
    n_chunks = ((j + 1) * tq + sc - 1) // sc

    _stack_heads(iq_ref, iqs_ref, tq)
    _stack_heads(q_ref, qs_ref, tq)
    _spread_weights(iw_ref[...], wrep_ref)
    row = lax.broadcasted_iota(I32, (tq, 1), 0)
    limit = j * tq + (row // CHUNK + 1) * CHUNK

    def score_chunk(c, carry):
        off = pl.multiple_of(c * sc, sc)
        acc = _index_scores(iqs_ref, wrep_ref, ik2_ref[pl.ds(off, sc), :])
        sidx = off + lax.broadcasted_iota(I32, (tq, sc), 1)
        key_ref[c] = jnp.where(sidx < limit, _sortable(acc), INT_MIN)
        return carry
    lax.fori_loop(0, n_chunks, score_chunk, 0)

    def count_ge(cand):
        def body(c, acc):
            return acc + _lane_fold(jnp.where(key_ref[c] >= cand, 1.0, 0.0))
        part = lax.fori_loop(0, n_chunks, body, jnp.zeros((tq, LANES), F32))
        return jnp.sum(part, axis=1, keepdims=True)

    t, n_ge = _select_threshold(count_ge, limit.astype(F32), topk)
    has_ties = jnp.max(n_ge) > topk

    @pl.when(jnp.logical_not(has_ties))
    def _():
        def body(c, carry):
            key_ref[c] = lax.bitcast_convert_type(_bias_plain(key_ref[c], t), I32)
            return carry
        lax.fori_loop(0, n_chunks, body, 0)

    @pl.when(has_ties)
    def _():
        need = topk - count_ge(t + 1)
        def body(c, run):
            bias, run = _bias_ties(key_ref[c], t, need, run)
            key_ref[c] = lax.bitcast_convert_type(bias, I32)
            return run
        lax.fori_loop(0, n_chunks, body, jnp.zeros((tq, 1), F32))

    _init_attend(m_ref, acc_ref)

    def attend(c, carry):
        off = pl.multiple_of(c * sc, sc)
        bias = lax.bitcast_convert_type(key_ref[c], F32)
        _attend_chunk(qs_ref, bias, kb_ref[pl.ds(off, sc), :], vb_ref[pl.ds(off, sc), :],
                      m_ref, acc_ref)
        return carry
    lax.fori_loop(0, n_chunks, attend, 0)
    _finish_attend(o_ref, acc_ref)


def _attn_scratch(rows):
    pairs = N_HEADS // 2
    return [
        pltpu.VMEM((N_HEADS * rows, LANES), BF16),
        pltpu.VMEM((N_HEADS * rows, LANES), BF16),
        pltpu.VMEM((IDX_HEADS, rows, LANES), F32),
        pltpu.VMEM((pairs, 2 * rows, LANES), F32),
        pltpu.VMEM((pairs, 2 * rows, 2 * LANES), F32),
    ]


def _prompt_attention(q, iq, iw, kb, vb, ik2, topk, tq=128, sc=512):
    s = kb.shape[0]
    row = lambda width: pl.BlockSpec((tq, width), lambda j: (j, 0))
    whole = pl.BlockSpec(memory_space=pltpu.VMEM)
    return pl.pallas_call(
        functools.partial(_prompt_attn_kernel, tq=tq, sc=sc, topk=topk),
        out_shape=jax.ShapeDtypeStruct((s, ATTN_DIM), BF16),
        grid=(s // tq,),
        in_specs=[row(ATTN_DIM), row(ATTN_DIM), row(LANES), whole, whole, whole],
        out_specs=row(ATTN_DIM),
        scratch_shapes=[pltpu.VMEM((s // sc, tq, sc), I32)] + _attn_scratch(tq),
        compiler_params=pltpu.CompilerParams(dimension_semantics=("arbitrary",),
                                             vmem_limit_bytes=VMEM_LIMIT),
        name="prompt_attention",
    )(q, iq, iw, kb, vb, ik2)


def _sample_attn_kernel(q_ref, iq_ref, iw_ref, kn_ref, vn_ref, ikn_ref, ck_ref, cv_ref, cik_ref, o_ref,
                        key_ref, keyn_ref, iqs_ref, qs_ref, wrep_ref, m_ref, acc_ref, *, sc, topk):
    ts = q_ref.shape[0]
    past = ck_ref.shape[1]
    n_chunks = past // sc

    _stack_heads(iq_ref, iqs_ref, ts)
    _stack_heads(q_ref, qs_ref, ts)
    _spread_weights(iw_ref[...], wrep_ref)

    pad = lambda x: jnp.concatenate([x, jnp.zeros((LANES - ts, x.shape[1]), x.dtype)], axis=0)
    new_ok = lax.broadcasted_iota(I32, (ts, LANES), 1) < ts

    def cache_ik(c):
        ik = cik_ref[0, c * sc:(c + 1) * sc, :].astype(BF16)
        return jnp.concatenate([ik, ik], axis=1)

    for c in range(n_chunks):
        key_ref[c] = _sortable(_index_scores(iqs_ref, wrep_ref,cache_ik(c)))
    keyn_ref[...] = jnp.where(new_ok, _sortable(_index_scores(iqs_ref, wrep_ref,pad(ikn_ref[...]))), INT_MIN)

    def count_ge(cand):
        part = jnp.where(keyn_ref[...] >= cand, 1.0, 0.0)
        for c in range(n_chunks):
            part = part + _lane_fold(jnp.where(key_ref[c] >= cand, 1.0, 0.0))
        return jnp.sum(part, axis=1, keepdims=True)

    t, n_ge = _select_threshold(count_ge, jnp.full((ts, 1), past + ts, F32), topk)
    has_ties = jnp.max(n_ge) > topk

    @pl.when(jnp.logical_not(has_ties))
    def _():
        for c in range(n_chunks):
            key_ref[c] = lax.bitcast_convert_type(_bias_plain(key_ref[c], t), I32)
        keyn_ref[...] = lax.bitcast_convert_type(_bias_plain(keyn_ref[...], t), I32)

    @pl.when(has_ties)
    def _():
        need = topk - count_ge(t + 1)
        run = jnp.zeros((ts, 1), F32)
        for c in range(n_chunks):
            bias, run = _bias_ties(key_ref[c], t, need, run)
            key_ref[c] = lax.bitcast_convert_type(bias, I32)
        bias, run = _bias_ties(keyn_ref[...], t, need, run)
        keyn_ref[...] = lax.bitcast_convert_type(bias, I32)

    _init_attend(m_ref, acc_ref)
    for c in range(n_chunks):
        _attend_chunk(qs_ref,lax.bitcast_convert_type(key_ref[c], F32),
                      ck_ref[0, c * sc:(c + 1) * sc, :].astype(BF16),
                      cv_ref[0, c * sc:(c + 1) * sc, :].astype(BF16), m_ref, acc_ref)
    _attend_chunk(qs_ref,lax.bitcast_convert_type(keyn_ref[...], F32),
                  pad(kn_ref[...]), pad(vn_ref[...]), m_ref, acc_ref)
    _finish_attend(o_ref, acc_ref)


def _sample_attention(q, iq, iw, kb, vb, ik2, cache_k, cache_v, cache_ik, ts, topk, sc=512):
    t = q.shape[0]
    nb, past = cache_k.shape[:2]
    row = lambda width: pl.BlockSpec((ts, width), lambda b: (b, 0))
    cache = lambda width: pl.BlockSpec((1, past, width), lambda b: (b, 0, 0))
    return pl.pallas_call(
        functools.partial(_sample_attn_kernel, sc=sc, topk=topk),
        out_shape=jax.ShapeDtypeStruct((t, ATTN_DIM), BF16),
        grid=(nb,),
        in_specs=[row(ATTN_DIM), row(ATTN_DIM), row(LANES), row(ATTN_DIM), row(ATTN_DIM), row(LANES),
                  cache(ATTN_DIM), cache(ATTN_DIM), cache(IDX_DIM)],
        out_specs=row(ATTN_DIM),
        scratch_shapes=[pltpu.VMEM((past // sc, ts, sc), I32), pltpu.VMEM((ts, LANES), I32)] + _attn_scratch(ts),
        compiler_params=pltpu.CompilerParams(dimension_semantics=("parallel",),
                                             vmem_limit_bytes=VMEM_LIMIT),
        name="sample_attention",
    )(q, iq, iw, kb, vb, ik2, cache_k, cache_v, cache_ik)


HIST_ROW0 = 2


def _merge_kernel(x_ref, a_ref, hist_ref, g_ref, wglu_ref, wgate_ref, bgate_ref, wa_ref,
                  dww_ref, dwb_ref, lng_ref, lnb_ref, wc_ref, wo_ref, postg_ref,
                  y_ref, conv_o_ref, uext_ref, *, nseq, seqlen, carry):
    i = pl.program_id(0)
    x = x_ref[...]
    n = _rms(x, g_ref[...]).astype(BF16)
    glu = jnp.dot(n, wglu_ref[...], preferred_element_type=F32)
    u = glu[:, :CONV_CH] * _sigmoid(glu[:, CONV_CH:])

    new0 = HIST_ROW0 + HIST
    if carry:
        @pl.when(i == 0)
        def _():
            uext_ref[:, HIST_ROW0:new0, :] = hist_ref[...]

        @pl.when(i > 0)
        def _():
            uext_ref[:, HIST_ROW0:new0, :] = uext_ref[:, seqlen + HIST_ROW0:seqlen + new0, :]
    else:
        uext_ref[:, HIST_ROW0:new0, :] = hist_ref[...]
    uext_ref[:, new0:new0 + seqlen, :] = u.reshape(nseq, seqlen, CONV_CH)
    conv_o_ref[...] = uext_ref[:, seqlen + HIST_ROW0:seqlen + new0, :]

    dww = dww_ref[...]
    cdw = None
    for tap in range(CONV_WIDTH):
        term = uext_ref[:, HIST_ROW0 + tap:HIST_ROW0 + tap + seqlen, :] * dww[tap:tap + 1, :]
        cdw = term if cdw is None else cdw + term
    cdw = cdw.reshape(nseq * seqlen, CONV_CH) + dwb_ref[...]

    mu = jnp.mean(cdw, axis=-1, keepdims=True)
    xc = cdw - mu
    var = jnp.mean(xc * xc, axis=-1, keepdims=True)
    ln = xc * lax.rsqrt(var + EPS) * lng_ref[...] + lnb_ref[...]
    cact = (ln * _sigmoid(ln)).astype(BF16)
    c_proj = jnp.dot(cact, wc_ref[...], preferred_element_type=F32)
    a_proj = jnp.dot(a_ref[...], wa_ref[...], preferred_element_type=F32)

    d = x.shape[1]
    gate = _sigmoid(jnp.dot(n, wgate_ref[...], preferred_element_type=F32) + bgate_ref[...])
    merged = (gate[:, :d] * a_proj + gate[:, d:] * c_proj).astype(BF16)
    y = jnp.dot(merged, wo_ref[...], preferred_element_type=F32)
    y_ref[...] = x + _rms(y, postg_ref[...])


def _branch_merge(x, a, hist, g, wglu, wgate, bgate, wa, dww, dwb, lng, lnb, wc, wo, postg,
                  *, nseq, seqlen, carry):
    t, d = x.shape
    tm = nseq * seqlen
    row = lambda width: pl.BlockSpec((tm, width), lambda i: (i, 0))
    if carry:
        seq_spec = pl.BlockSpec((nseq, HIST, CONV_CH), lambda i: (0, 0, 0))
        n_state = nseq
    else:
        seq_spec = pl.BlockSpec((nseq, HIST, CONV_CH), lambda i: (i, 0, 0))
        n_state = t // seqlen
    consts = [g, wglu, wgate, bgate, wa, dww, dwb, lng, lnb, wc, wo, postg]
    return pl.pallas_call(
        functools.partial(_merge_kernel, nseq=nseq, seqlen=seqlen, carry=carry),
        out_shape=[jax.ShapeDtypeStruct((t, d), F32),
                   jax.ShapeDtypeStruct((n_state, HIST, CONV_CH), F32)],
        grid=(t // tm,),
        in_specs=[row(d), row(ATTN_DIM), seq_spec] + [_const_spec(c.shape) for c in consts],
        out_specs=[row(d), seq_spec],
        scratch_shapes=[pltpu.VMEM((nseq, HIST_ROW0 + HIST + seqlen, CONV_CH), F32)],
        compiler_params=pltpu.CompilerParams(dimension_semantics=("arbitrary",),
                                             vmem_limit_bytes=VMEM_LIMIT),
        name="branch_merge",
    )(x, a, hist, *consts)


def kernel(x_prompt, x_sample, cache_k, cache_v, cache_idx_k, state_conv,
           ffn1_pre_g, ffn1_w_gate, ffn1_w_up, ffn1_w_down, ffn1_post_g,
           mix_pre_g, w_in, b_gate, w_attn_o, conv_dw_w, conv_dw_b, conv_ln_g, conv_ln_b,
           w_conv_o, w_out, mix_post_g,
           ffn2_pre_g, ffn2_w_gate, ffn2_w_up, ffn2_w_down, ffn2_post_g):
    bp, lp, d = x_prompt.shape
    db, ts, _ = x_sample.shape
    depth, _, past = cache_k.shape[:3]
    assert bp == 1 and depth == 1
    topk_p = min(TOPK_MAX, lp // 4)
    topk_s = min(TOPK_MAX, (past + ts) // 4)

    vec = lambda a: a.reshape(1, -1)
    bf = lambda a: a.astype(BF16)
    ffn1 = (vec(ffn1_pre_g[0]), bf(ffn1_w_gate[0]), bf(ffn1_w_up[0]), bf(ffn1_w_down[0]), vec(ffn1_post_g[0]))
    ffn2 = (vec(ffn2_pre_g[0]), bf(ffn2_w_gate[0]), bf(ffn2_w_up[0]), bf(ffn2_w_down[0]), vec(ffn2_post_g[0]))

    widths = (IDX_HEADS * IDX_DIM, IDX_HEADS, IDX_DIM, ATTN_DIM, ATTN_DIM, ATTN_DIM, 2 * CONV_CH, 2 * d)
    offs = [0]
    for wd_ in widths:
        offs.append(offs[-1] + wd_)
    wi = w_in[0]
    col = lambda k: wi[:, offs[k]:offs[k + 1]]
    w_iw = jnp.pad(col(1), ((0, 0), (0, LANES - IDX_HEADS)))
    w2 = bf(jnp.concatenate([col(0), col(3), col(4), col(5), col(2), col(2), w_iw], axis=1))
    merge_w = (vec(mix_pre_g[0]), bf(col(6)), bf(col(7)), vec(b_gate[0]), bf(w_attn_o[0]),
               conv_dw_w[0], vec(conv_dw_b[0]), vec(conv_ln_g[0]), vec(conv_ln_b[0]),
               bf(w_conv_o[0]), bf(w_out[0]), vec(mix_post_g[0]))

    pos_p = jnp.arange(lp, dtype=I32)
    pos_s = jnp.tile(past + jnp.arange(ts, dtype=I32), db)

    hp = _ffn_half(x_prompt.reshape(lp, d), *ffn1)
    iq, q, kf, kb, vf, vb, ikf, ik2, iw = _mixer_proj(hp, vec(mix_pre_g[0]), w2, _rope_tables(pos_p))
    a = _prompt_attention(q, iq, iw, kb, vb, ik2, topk_p)
    hp, conv_p = _branch_merge(hp, a, jnp.zeros((1, HIST, CONV_CH), F32), *merge_w,
                               nseq=1, seqlen=256, carry=True)
    hp = _ffn_half(hp, *ffn2)
    outs_p = (kf.reshape(1, 1, lp, N_HEADS, HEAD_DIM), vf.reshape(1, 1, lp, N_HEADS, HEAD_DIM),
              ikf.reshape(1, 1, lp, IDX_DIM), conv_p.reshape(1, 1, HIST, CONV_CH))

    n_s = db * ts
    hs = _ffn_half(x_sample.reshape(n_s, d), *ffn1)
    iq, q, kf, kb, vf, vb, ikf, ik2, iw = _mixer_proj(hs, vec(mix_pre_g[0]), w2, _rope_tables(pos_s))
    a = _sample_attention(q, iq, iw, kb, vb, ik2, cache_k[0].reshape(db, past, ATTN_DIM),
                          cache_v[0].reshape(db, past, ATTN_DIM), cache_idx_k[0], ts, topk_s)
    hs, conv_s = _branch_merge(hs, a, state_conv[0], *merge_w, nseq=8, seqlen=ts, carry=False)
    hs = _ffn_half(hs, *ffn2)
    outs_s = (kf.reshape(1, db, ts, N_HEADS, HEAD_DIM), vf.reshape(1, db, ts, N_HEADS, HEAD_DIM),
              ikf.reshape(1, db, ts, IDX_DIM), conv_s.reshape(1, db, HIST, CONV_CH))

    return (hp.reshape(1, lp, d), hs.reshape(db, ts, d)) + outs_p + outs_s
```

```python
import functools
import math

import jax
import jax.numpy as jnp
from jax import lax
from jax.experimental import pallas as pl
from jax.experimental.pallas import tpu as pltpu

F32 = jnp.float32
BF16 = jnp.bfloat16
I32 = jnp.int32

CHUNK = 64
N_HEADS = 8
HEAD_DIM = 64
ATTN_DIM = N_HEADS * HEAD_DIM
IDX_HEADS = 8
IDX_DIM = 64
IDX_W_SCALE = (IDX_HEADS * IDX_DIM) ** -0.5
TOPK_MAX = 256
ROPE_THETA = 500000.0
ROT = HEAD_DIM // 4
CONV_CH = 512
CONV_WIDTH = 31
HIST = CONV_WIDTH - 1
EPS = 1e-6

LANES = 128
VMEM_LIMIT = 56 * 1024 * 1024
INT_MIN = -(2 ** 31)
NEG_BIG = -1e30
Q_SCALE = HEAD_DIM ** -0.5 * math.log2(math.e)

_NT = (((1,), (1,)), ((), ()))


def _rms(x, g):
    return x * lax.rsqrt(jnp.mean(x * x, axis=-1, keepdims=True) + EPS) * g


def _sigmoid(x):
    return 1.0 / (1.0 + jnp.exp(-x))


def _const_spec(shape):
    nd = len(shape)
    return pl.BlockSpec(shape, lambda *_: (0,) * nd, pipeline_mode=pl.Buffered(1))


def _ffn_kernel(x_ref, pre_g_ref, wg_ref, wu_ref, wd_ref, post_g_ref, o_ref):
    x = x_ref[...]
    h = _rms(x, pre_g_ref[...]).astype(BF16)
    g = jnp.dot(h, wg_ref[...], preferred_element_type=F32)
    u = jnp.dot(h, wu_ref[...], preferred_element_type=F32)
    a = (g * _sigmoid(g) * u).astype(BF16)
    f = jnp.dot(a, wd_ref[...], preferred_element_type=F32)
    o_ref[...] = x + 0.5 * _rms(f, post_g_ref[...])


def _ffn_half(x, pre_g, wg, wu, wd, post_g, tm=256):
    t, d = x.shape
    dff = wg.shape[1]
    row = pl.BlockSpec((tm, d), lambda i: (i, 0))
    return pl.pallas_call(
        _ffn_kernel,
        out_shape=jax.ShapeDtypeStruct((t, d), F32),
        grid=(t // tm,),
        in_specs=[row, _const_spec((1, d)), _const_spec((d, dff)), _const_spec((d, dff)),
                  _const_spec((dff, d)), _const_spec((1, d))],
        out_specs=row,
        compiler_params=pltpu.CompilerParams(dimension_semantics=("parallel",),
                                             vmem_limit_bytes=VMEM_LIMIT),
        name="ffn_half",
    )(x, pre_g, wg, wu, wd, post_g)


def _rope_tables(pos):
    half = ROT // 2
    inv_freq = jnp.exp(-math.log(ROPE_THETA) * jnp.arange(half, dtype=F32) * (2.0 / ROT))
    ang = pos.astype(F32)[:, None] * inv_freq[None, :]
    cos, sin = jnp.cos(ang), jnp.sin(ang)
    t = pos.shape[0]
    ones = jnp.ones((t, HEAD_DIM - ROT), F32)
    c = jnp.concatenate([cos, cos, ones], axis=1)
    s_up = jnp.concatenate([-sin, jnp.zeros((t, HEAD_DIM - half), F32)], axis=1)
    s_dn = jnp.concatenate([jnp.zeros((t, half), F32), sin, jnp.zeros((t, HEAD_DIM - ROT), F32)], axis=1)
    dup = lambda a: jnp.concatenate([a, a], axis=1)
    return dup(c), dup(s_up), dup(s_dn)


def _proj_kernel(x_ref, g_ref, w_ref, c_ref, sup_ref, sdn_ref,
                 iq_o, q_o, kf_o, kb_o, vf_o, vb_o, ikf_o, ik2_o, iw_o):
    n = _rms(x_ref[...], g_ref[...]).astype(BF16)
    p = jnp.dot(n, w_ref[...], preferred_element_type=F32)
    c, s_up, s_dn = c_ref[...], sup_ref[...], sdn_ref[...]

    def rope(x):
        return x * c + pltpu.roll(x, LANES - ROT // 2, 1) * s_up + pltpu.roll(x, ROT // 2, 1) * s_dn

    a = ATTN_DIM
    for gidx in range(a // LANES):
        sl = slice(gidx * LANES, (gidx + 1) * LANES)
        iq_o[:, sl] = rope(p[:, sl]).astype(BF16)
        q_o[:, sl] = (rope(p[:, a + gidx * LANES:a + (gidx + 1) * LANES]) * Q_SCALE).astype(BF16)
        k = rope(p[:, 2 * a + gidx * LANES:2 * a + (gidx + 1) * LANES])
        kf_o[:, sl] = k
        kb_o[:, sl] = k.astype(BF16)
        v = p[:, 3 * a + gidx * LANES:3 * a + (gidx + 1) * LANES]
        vf_o[:, sl] = v
        vb_o[:, sl] = v.astype(BF16)
    ik = rope(p[:, 4 * a:4 * a + LANES])
    ikf_o[...] = ik[:, :IDX_DIM]
    ik2_o[...] = ik.astype(BF16)
    iw_o[...] = p[:, 4 * a + LANES:4 * a + 2 * LANES] * IDX_W_SCALE


def _mixer_proj(x, g, w2, tables, tm=256):
    t, d = x.shape
    ncol = w2.shape[1]
    row = lambda width: pl.BlockSpec((tm, width), lambda i: (i, 0))
    a = ATTN_DIM
    outs = [
        jax.ShapeDtypeStruct((t, a), BF16),
        jax.ShapeDtypeStruct((t, a), BF16),
        jax.ShapeDtypeStruct((t, a), F32),
        jax.ShapeDtypeStruct((t, a), BF16),
        jax.ShapeDtypeStruct((t, a), F32),
        jax.ShapeDtypeStruct((t, a), BF16),
        jax.ShapeDtypeStruct((t, IDX_DIM), F32),
        jax.ShapeDtypeStruct((t, LANES), BF16),
        jax.ShapeDtypeStruct((t, LANES), F32),
    ]
    return pl.pallas_call(
        _proj_kernel,
        out_shape=outs,
        grid=(t // tm,),
        in_specs=[row(d), _const_spec((1, d)), _const_spec((d, ncol)), row(LANES), row(LANES), row(LANES)],
        out_specs=[row(a), row(a), row(a), row(a), row(a), row(a), row(IDX_DIM), row(LANES), row(LANES)],
        compiler_params=pltpu.CompilerParams(dimension_semantics=("parallel",),
                                             vmem_limit_bytes=VMEM_LIMIT),
        name="mixer_proj",
    )(x, g, w2, *tables)


def _sortable(x):
    b = lax.bitcast_convert_type(x, I32)
    k = b ^ ((b >> 31) & 0x7FFFFFFF)
    return jnp.where(x == 0.0, 0, k)


def _head_lhs(x, h):
    grp, odd = divmod(h, 2)
    xg = x[:, grp * LANES:(grp + 1) * LANES].astype(F32)
    lane = lax.broadcasted_iota(I32, xg.shape, 1)
    keep = (lane >= HEAD_DIM) if odd else (lane < HEAD_DIM)
    return jnp.where(keep, xg, 0.0).astype(BF16)


def _stack_heads(x_ref, dst_ref, rows):
    x = x_ref[...]
    for h in range(N_HEADS):
        dst_ref[h * rows:(h + 1) * rows, :] = _head_lhs(x, h)


def _spread_weights(w, wrep_ref):
    for h in range(IDX_HEADS):
        wrep_ref[h] = jnp.broadcast_to(w[:, h:h + 1], wrep_ref.shape[1:])


def _index_scores(iqs_ref, wrep_ref, ikc):
    rows = wrep_ref.shape[1]
    d = lax.dot_general(iqs_ref[...], ikc, _NT, preferred_element_type=F32)
    groups = []
    for l in range(ikc.shape[0] // LANES):
        acc = None
        for h in range(IDX_HEADS):
            term = jnp.maximum(d[h * rows:(h + 1) * rows, l * LANES:(l + 1) * LANES], 0.0) * wrep_ref[h]
            acc = term if acc is None else acc + term
        groups.append(acc)
    return jnp.concatenate(groups, axis=1)


def _lane_fold(x):
    out = x[:, :LANES]
    for l in range(1, x.shape[1] // LANES):
        out = out + x[:, l * LANES:(l + 1) * LANES]
    return out


COUNT_ROWS = 32


def _count_tile(tile_ref, cand):
    rows = tile_ref.shape[0]
    step = min(COUNT_ROWS, rows)
    parts = []
    for r in range(0, rows, step):
        hit = jnp.where(tile_ref[r:r + step, :] >= cand[r:r + step], 1.0, 0.0)
        parts.append(_lane_fold(hit))
    return jnp.concatenate(parts, axis=0)


def _select_threshold(count_ge, n_adm, topk):
    c0 = count_ge(jnp.zeros(n_adm.shape, I32))
    c1 = count_ge(jnp.ones(n_adm.shape, I32))
    nonneg = c0 >= topk
    zero_tie = jnp.logical_and(nonneg, c1 < topk)

    def cond(st):
        i, _, cnt = st
        return jnp.logical_and(i < 32, jnp.max(jnp.where(zero_tie, 0.0, cnt)) > topk)

    def one_bit(i, t, cnt):
        cand = t + jnp.left_shift(jnp.int32(1), 31 - i)
        c = count_ge(cand)
        ok = c >= topk
        return jnp.where(ok, cand, t), jnp.where(ok, c, cnt)

    def step(st):
        i, t, cnt = st
        t, cnt = one_bit(i, t, cnt)
        t, cnt = one_bit(i + 1, t, cnt)
        return i + 2, t, cnt

    t, cnt = one_bit(1, jnp.where(nonneg, 0, INT_MIN), jnp.where(nonneg, c0, n_adm))
    _, t, cnt = lax.while_loop(cond, step, (jnp.int32(2), t, cnt))
    return jnp.maximum(t, INT_MIN + 1), cnt


def _bias_plain(key, t):
    return jnp.where(key >= t, 0.0, NEG_BIG)


def _init_tri(tri_ref):
    @pl.when(pl.program_id(0) == 0)
    def _():
        r = lax.broadcasted_iota(I32, tri_ref.shape, 0)
        c = lax.broadcasted_iota(I32, tri_ref.shape, 1)
        tri_ref[...] = jnp.where(r < c, 1.0, 0.0).astype(BF16)


def _bias_ties(key, t, need, run, tri_ref):
    sc = key.shape[1]
    eq = key == t
    eqf = jnp.where(eq, 1.0, 0.0)
    before = run + jnp.dot(eqf.astype(BF16), tri_ref[:sc, :sc], preferred_element_type=F32)
    sel = (key > t) | (eq & (before < need))
    return jnp.where(sel, 0.0, NEG_BIG), run + jnp.sum(eqf, axis=1, keepdims=True)


def _attend_chunk(qs_ref, bias, kc, vc, m_ref, acc_ref):
    pairs = N_HEADS // 2
    prow = qs_ref.shape[0] // pairs
    nrep = kc.shape[0] // LANES
    bias2 = jnp.concatenate([bias, bias], axis=0)
    ones = jnp.ones((vc.shape[0], LANES), BF16)
    scores = [lax.dot_general(qs_ref[g * prow:(g + 1) * prow, :], kc[:, g * LANES:(g + 1) * LANES], _NT,
                              preferred_element_type=F32) + bias2 for g in range(pairs)]
    probs, alphas = [], []
    for g in range(pairs):
        s = scores[g]
        m_prev = m_ref[g]
        m_next = jnp.maximum(m_prev, jnp.max(s, axis=1, keepdims=True))
        alphas.append(jnp.exp2(m_prev - m_next))
        probs.append(jnp.exp2(s - jnp.concatenate([m_next] * nrep, axis=1)).astype(BF16))
        m_ref[g] = m_next
    for g in range(pairs):
        vext = jnp.concatenate([vc[:, g * LANES:(g + 1) * LANES], ones], axis=1)
        pv = jnp.dot(probs[g], vext, preferred_element_type=F32)
        acc_ref[g] = jnp.concatenate([alphas[g], alphas[g]], axis=1) * acc_ref[g] + pv


def _init_attend(m_ref, acc_ref):
    m_ref[...] = jnp.full(m_ref.shape, NEG_BIG, F32)
    acc_ref[...] = jnp.zeros(acc_ref.shape, F32)


def _finish_attend(o_ref, acc_ref):
    rows = acc_ref.shape[1] // 2
    lane = lax.broadcasted_iota(I32, (rows, LANES), 1)
    for g in range(N_HEADS // 2):
        acc = acc_ref[g]
        o = acc[:, :LANES] / acc[:, LANES:]
        o_ref[:, g * LANES:(g + 1) * LANES] = jnp.where(lane < HEAD_DIM, o[:rows], o[rows:]).astype(o_ref.dtype)


def _prompt_attn_kernel(q_ref, iq_ref, iw_ref, kb_ref, vb_ref, ik2_ref, o_ref,
                        key_ref, iqs_ref, qs_ref, wrep_ref, m_ref, acc_ref, tri_ref, *, tq, sc, topk):
    j = pl.program_id(0)
    n_chunks = ((j + 1) * tq + sc - 1) // sc

    _init_tri(tri_ref)
    _stack_heads(iq_ref, iqs_ref, tq)
    _stack_heads(q_ref, qs_ref, tq)
    _spread_weights(iw_ref[...], wrep_ref)
    row = lax.broadcasted_iota(I32, (tq, 1), 0)
    limit = j * tq + (row // CHUNK + 1) * CHUNK

    def score_chunk(c, carry):
        off = pl.multiple_of(c * sc, sc)
        acc = _index_scores(iqs_ref, wrep_ref, ik2_ref[pl.ds(off, sc), :])
        sidx = off + lax.broadcasted_iota(I32, (tq, sc), 1)
        key_ref[c] = jnp.where(sidx < limit, _sortable(acc), INT_MIN)
        return carry
    lax.fori_loop(0, n_chunks, score_chunk, 0)

    def count_ge(cand):
        def body(c, acc):
            return acc + _count_tile(key_ref.at[c], cand)
        part = lax.fori_loop(0, n_chunks, body, jnp.zeros((tq, LANES), F32))
        return jnp.sum(part, axis=1, keepdims=True)

    t, n_ge = _select_threshold(count_ge, limit.astype(F32), topk)
    has_ties = jnp.max(n_ge) > topk

    @pl.when(jnp.logical_not(has_ties))
    def _():
        def body(c, carry):
            key_ref[c] = lax.bitcast_convert_type(_bias_plain(key_ref[c], t), I32)
            return carry
        lax.fori_loop(0, n_chunks, body, 0)

    @pl.when(has_ties)
    def _():
        need = topk - count_ge(t + 1)
        def body(c, run):
            bias, run = _bias_ties(key_ref[c], t, need, run, tri_ref)
            key_ref[c] = lax.bitcast_convert_type(bias, I32)
            return run
        lax.fori_loop(0, n_chunks, body, jnp.zeros((tq, 1), F32))

    _init_attend(m_ref, acc_ref)

    def attend(c, carry):
        off = pl.multiple_of(c * sc, sc)
        bias = lax.bitcast_convert_type(key_ref[c], F32)
        _attend_chunk(qs_ref, bias, kb_ref[pl.ds(off, sc), :], vb_ref[pl.ds(off, sc), :],
                      m_ref, acc_ref)
        return carry
    lax.fori_loop(0, n_chunks, attend, 0)
    _finish_attend(o_ref, acc_ref)


def _attn_scratch(rows, sc):
    pairs = N_HEADS // 2
    return [
        pltpu.VMEM((N_HEADS * rows, LANES), BF16),
        pltpu.VMEM((N_HEADS * rows, LANES), BF16),
        pltpu.VMEM((IDX_HEADS, rows, LANES), F32),
        pltpu.VMEM((pairs, 2 * rows, LANES), F32),
        pltpu.VMEM((pairs, 2 * rows, 2 * LANES), F32),
        pltpu.VMEM((sc, sc), BF16),
    ]


def _prompt_attention(q, iq, iw, kb, vb, ik2, topk, tq=128, sc=512):
    s = kb.shape[0]
    row = lambda width: pl.BlockSpec((tq, width), lambda j: (j, 0))
    whole = pl.BlockSpec(memory_space=pltpu.VMEM)
    return pl.pallas_call(
        functools.partial(_prompt_attn_kernel, tq=tq, sc=sc, topk=topk),
        out_shape=jax.ShapeDtypeStruct((s, ATTN_DIM), BF16),
        grid=(s // tq,),
        in_specs=[row(ATTN_DIM), row(ATTN_DIM), row(LANES), whole, whole, whole],
        out_specs=row(ATTN_DIM),
        scratch_shapes=[pltpu.VMEM((s // sc, tq, sc), I32)] + _attn_scratch(tq, sc),
        compiler_params=pltpu.CompilerParams(dimension_semantics=("arbitrary",),
                                             vmem_limit_bytes=VMEM_LIMIT),
        name="prompt_attention",
    )(q, iq, iw, kb, vb, ik2)


def _sample_attn_kernel(q_ref, iq_ref, iw_ref, kn_ref, vn_ref, ikn_ref, ck_ref, cv_ref, cik_ref, o_ref,
                        key_ref, keyn_ref, iqs_ref, qs_ref, wrep_ref, m_ref, acc_ref, tri_ref, *, sc, topk):
    ts = q_ref.shape[0]
    past = ck_ref.shape[1]
    n_chunks = past // sc

    _init_tri(tri_ref)
    _stack_heads(iq_ref, iqs_ref, ts)
    _stack_heads(q_ref, qs_ref, ts)
    _spread_weights(iw_ref[...], wrep_ref)

    pad = lambda x: jnp.concatenate([x, jnp.zeros((LANES - ts, x.shape[1]), x.dtype)], axis=0)
    new_ok = lax.broadcasted_iota(I32, (ts, LANES), 1) < ts

    def cache_ik(c):
        ik = cik_ref[0, c * sc:(c + 1) * sc, :].astype(BF16)
        return jnp.concatenate([ik, ik], axis=1)

    for c in range(n_chunks):
        key_ref[c] = _sortable(_index_scores(iqs_ref, wrep_ref,cache_ik(c)))
    keyn_ref[...] = jnp.where(new_ok, _sortable(_index_scores(iqs_ref, wrep_ref,pad(ikn_ref[...]))), INT_MIN)

    def count_ge(cand):
        part = jnp.where(keyn_ref[...] >= cand, 1.0, 0.0)
        for c in range(n_chunks):
            part = part + _count_tile(key_ref.at[c], cand)
        return jnp.sum(part, axis=1, keepdims=True)

    t, n_ge = _select_threshold(count_ge, jnp.full((ts, 1), past + ts, F32), topk)
    has_ties = jnp.max(n_ge) > topk

    @pl.when(jnp.logical_not(has_ties))
    def _():
        for c in range(n_chunks):
            key_ref[c] = lax.bitcast_convert_type(_bias_plain(key_ref[c], t), I32)
        keyn_ref[...] = lax.bitcast_convert_type(_bias_plain(keyn_ref[...], t), I32)

    @pl.when(has_ties)
    def _():
        need = topk - count_ge(t + 1)
        run = jnp.zeros((ts, 1), F32)
        for c in range(n_chunks):
            bias, run = _bias_ties(key_ref[c], t, need, run, tri_ref)
            key_ref[c] = lax.bitcast_convert_type(bias, I32)
        bias, run = _bias_ties(keyn_ref[...], t, need, run, tri_ref)
        keyn_ref[...] = lax.bitcast_convert_type(bias, I32)

    _init_attend(m_ref, acc_ref)
    for c in range(n_chunks):
        _attend_chunk(qs_ref,lax.bitcast_convert_type(key_ref[c], F32),
                      ck_ref[0, c * sc:(c + 1) * sc, :].astype(BF16),
                      cv_ref[0, c * sc:(c + 1) * sc, :].astype(BF16), m_ref, acc_ref)
    _attend_chunk(qs_ref,lax.bitcast_convert_type(keyn_ref[...], F32),
                  pad(kn_ref[...]), pad(vn_ref[...]), m_ref, acc_ref)
    _finish_attend(o_ref, acc_ref)


def _sample_attention(q, iq, iw, kb, vb, ik2, cache_k, cache_v, cache_ik, ts, topk, sc=512):
    t = q.shape[0]
    nb, past = cache_k.shape[:2]
    row = lambda width: pl.BlockSpec((ts, width), lambda b: (b, 0))
    cache = lambda width: pl.BlockSpec((1, past, width), lambda b: (b, 0, 0))
    return pl.pallas_call(
        functools.partial(_sample_attn_kernel, sc=sc, topk=topk),
        out_shape=jax.ShapeDtypeStruct((t, ATTN_DIM), BF16),
        grid=(nb,),
        in_specs=[row(ATTN_DIM), row(ATTN_DIM), row(LANES), row(ATTN_DIM), row(ATTN_DIM), row(LANES),
                  cache(ATTN_DIM), cache(ATTN_DIM), cache(IDX_DIM)],
        out_specs=row(ATTN_DIM),
        scratch_shapes=([pltpu.VMEM((past // sc, ts, sc), I32), pltpu.VMEM((ts, LANES), I32)]
                        + _attn_scratch(ts, sc)),
        compiler_params=pltpu.CompilerParams(dimension_semantics=("arbitrary",),
                                             vmem_limit_bytes=VMEM_LIMIT),
        name="sample_attention",
    )(q, iq, iw, kb, vb, ik2, cache_k, cache_v, cache_ik)


HIST_ROW0 = 2


def _merge_kernel(x_ref, a_ref, hist_ref, g_ref, wglu_ref, wgate_ref, bgate_ref, wa_ref,
                  dww_ref, dwb_ref, lng_ref, lnb_ref, wc_ref, wo_ref, postg_ref,
                  y_ref, conv_o_ref, uext_ref, *, nseq, seqlen, carry):
    i = pl.program_id(0)
    x = x_ref[...]
    n = _rms(x, g_ref[...]).astype(BF16)
    glu = jnp.dot(n, wglu_ref[...], preferred_element_type=F32)
    u = glu[:, :CONV_CH] * _sigmoid(glu[:, CONV_CH:])

    new0 = HIST_ROW0 + HIST
    if carry:
        @pl.when(i == 0)
        def _():
            uext_ref[:, HIST_ROW0:new0, :] = hist_ref[...]

        @pl.when(i > 0)
        def _():
            uext_ref[:, HIST_ROW0:new0, :] = uext_ref[:, seqlen + HIST_ROW0:seqlen + new0, :]
    else:
        uext_ref[:, HIST_ROW0:new0, :] = hist_ref[...]
    uext_ref[:, new0:new0 + seqlen, :] = u.reshape(nseq, seqlen, CONV_CH)
    conv_o_ref[...] = uext_ref[:, seqlen + HIST_ROW0:seqlen + new0, :]

    dww = dww_ref[...]
    cdw = None
    for tap in range(CONV_WIDTH):
        term = uext_ref[:, HIST_ROW0 + tap:HIST_ROW0 + tap + seqlen, :] * dww[tap:tap + 1, :]
        cdw = term if cdw is None else cdw + term
    cdw = cdw.reshape(nseq * seqlen, CONV_CH) + dwb_ref[...]

    mu = jnp.mean(cdw, axis=-1, keepdims=True)
    xc = cdw - mu
    var = jnp.mean(xc * xc, axis=-1, keepdims=True)
    ln = xc * lax.rsqrt(var + EPS) * lng_ref[...] + lnb_ref[...]
    cact = (ln * _sigmoid(ln)).astype(BF16)
    c_proj = jnp.dot(cact, wc_ref[...], preferred_element_type=F32)
    a_proj = jnp.dot(a_ref[...], wa_ref[...], preferred_element_type=F32)

    d = x.shape[1]
    gate = _sigmoid(jnp.dot(n, wgate_ref[...], preferred_element_type=F32) + bgate_ref[...])
    merged = (gate[:, :d] * a_proj + gate[:, d:] * c_proj).astype(BF16)
    y = jnp.dot(merged, wo_ref[...], preferred_element_type=F32)
    y_ref[...] = x + _rms(y, postg_ref[...])


def _branch_merge(x, a, hist, g, wglu, wgate, bgate, wa, dww, dwb, lng, lnb, wc, wo, postg,
                  *, nseq, seqlen, carry):
    t, d = x.shape
    tm = nseq * seqlen
    row = lambda width: pl.BlockSpec((tm, width), lambda i: (i, 0))
    if carry:
        seq_spec = pl.BlockSpec((nseq, HIST, CONV_CH), lambda i: (0, 0, 0))
        n_state = nseq
    else:
        seq_spec = pl.BlockSpec((nseq, HIST, CONV_CH), lambda i: (i, 0, 0))
        n_state = t // seqlen
    consts = [g, wglu, wgate, bgate, wa, dww, dwb, lng, lnb, wc, wo, postg]
    return pl.pallas_call(
        functools.partial(_merge_kernel, nseq=nseq, seqlen=seqlen, carry=carry),
        out_shape=[jax.ShapeDtypeStruct((t, d), F32),
                   jax.ShapeDtypeStruct((n_state, HIST, CONV_CH), F32)],
        grid=(t // tm,),
        in_specs=[row(d), row(ATTN_DIM), seq_spec] + [_const_spec(c.shape) for c in consts],
        out_specs=[row(d), seq_spec],
        scratch_shapes=[pltpu.VMEM((nseq, HIST_ROW0 + HIST + seqlen, CONV_CH), F32)],
        compiler_params=pltpu.CompilerParams(dimension_semantics=("arbitrary",),
                                             vmem_limit_bytes=VMEM_LIMIT),
        name="branch_merge",
    )(x, a, hist, *consts)


def kernel(x_prompt, x_sample, cache_k, cache_v, cache_idx_k, state_conv,
           ffn1_pre_g, ffn1_w_gate, ffn1_w_up, ffn1_w_down, ffn1_post_g,
           mix_pre_g, w_in, b_gate, w_attn_o, conv_dw_w, conv_dw_b, conv_ln_g, conv_ln_b,
           w_conv_o, w_out, mix_post_g,
           ffn2_pre_g, ffn2_w_gate, ffn2_w_up, ffn2_w_down, ffn2_post_g):
    bp, lp, d = x_prompt.shape
    db, ts, _ = x_sample.shape
    depth, _, past = cache_k.shape[:3]
    assert bp == 1 and depth == 1
    topk_p = min(TOPK_MAX, lp // 4)
    topk_s = min(TOPK_MAX, (past + ts) // 4)

    vec = lambda a: a.reshape(1, -1)
    bf = lambda a: a.astype(BF16)
    ffn1 = (vec(ffn1_pre_g[0]), bf(ffn1_w_gate[0]), bf(ffn1_w_up[0]), bf(ffn1_w_down[0]), vec(ffn1_post_g[0]))
    ffn2 = (vec(ffn2_pre_g[0]), bf(ffn2_w_gate[0]), bf(ffn2_w_up[0]), bf(ffn2_w_down[0]), vec(ffn2_post_g[0]))

    widths = (IDX_HEADS * IDX_DIM, IDX_HEADS, IDX_DIM, ATTN_DIM, ATTN_DIM, ATTN_DIM, 2 * CONV_CH, 2 * d)
    offs = [0]
    for wd_ in widths:
        offs.append(offs[-1] + wd_)
    wi = w_in[0]
    col = lambda k: wi[:, offs[k]:offs[k + 1]]
    w_iw = jnp.pad(col(1), ((0, 0), (0, LANES - IDX_HEADS)))
    w2 = bf(jnp.concatenate([col(0), col(3), col(4), col(5), col(2), col(2), w_iw], axis=1))
    merge_w = (vec(mix_pre_g[0]), bf(col(6)), bf(col(7)), vec(b_gate[0]), bf(w_attn_o[0]),
               conv_dw_w[0], vec(conv_dw_b[0]), vec(conv_ln_g[0]), vec(conv_ln_b[0]),
               bf(w_conv_o[0]), bf(w_out[0]), vec(mix_post_g[0]))

    pos_p = jnp.arange(lp, dtype=I32)
    pos_s = jnp.tile(past + jnp.arange(ts, dtype=I32), db)

    hp = _ffn_half(x_prompt.reshape(lp, d), *ffn1)
    iq, q, kf, kb, vf, vb, ikf, ik2, iw = _mixer_proj(hp, vec(mix_pre_g[0]), w2, _rope_tables(pos_p))
    a = _prompt_attention(q, iq, iw, kb, vb, ik2, topk_p)
    hp, conv_p = _branch_merge(hp, a, jnp.zeros((1, HIST, CONV_CH), F32), *merge_w,
                               nseq=1, seqlen=256, carry=True)
    hp = _ffn_half(hp, *ffn2)
    outs_p = (kf.reshape(1, 1, lp, N_HEADS, HEAD_DIM), vf.reshape(1, 1, lp, N_HEADS, HEAD_DIM),
              ikf.reshape(1, 1, lp, IDX_DIM), conv_p.reshape(1, 1, HIST, CONV_CH))

    n_s = db * ts
    hs = _ffn_half(x_sample.reshape(n_s, d), *ffn1)
    iq, q, kf, kb, vf, vb, ikf, ik2, iw = _mixer_proj(hs, vec(mix_pre_g[0]), w2, _rope_tables(pos_s))
    a = _sample_attention(q, iq, iw, kb, vb, ik2, cache_k[0].reshape(db, past, ATTN_DIM),
                          cache_v[0].reshape(db, past, ATTN_DIM), cache_idx_k[0], ts, topk_s)
    hs, conv_s = _branch_merge(hs, a, state_conv[0], *merge_w, nseq=8, seqlen=ts, carry=False)
    hs = _ffn_half(hs, *ffn2)
    outs_s = (kf.reshape(1, db, ts, N_HEADS, HEAD_DIM), vf.reshape(1, db, ts, N_HEADS, HEAD_DIM),
              ikf.reshape(1, db, ts, IDX_DIM), conv_s.reshape(1, db, HIST, CONV_CH))

    return (hp.reshape(1, lp, d), hs.reshape(db, ts, d)) + outs_p + outs_s
```

```python
import functools
import math

import jax
import jax.numpy as jnp
from jax import lax
from jax.experimental import pallas as pl
from jax.experimental.pallas import tpu as pltpu

F32 = jnp.float32
BF16 = jnp.bfloat16
I32 = jnp.int32
I16 = jnp.int16

CHUNK = 64
N_HEADS = 8
HEAD_DIM = 64
ATTN_DIM = N_HEADS * HEAD_DIM
IDX_HEADS = 8
IDX_DIM = 64
IDX_W_SCALE = (IDX_HEADS * IDX_DIM) ** -0.5
TOPK_MAX = 256
ROPE_THETA = 500000.0
ROT = HEAD_DIM // 4
CONV_CH = 512
CONV_WIDTH = 31
HIST = CONV_WIDTH - 1
EPS = 1e-6

LANES = 128
VMEM_LIMIT = 56 * 1024 * 1024
INT_MIN = -(2 ** 31)
NEG_BIG = -1e30
Q_SCALE = HEAD_DIM ** -0.5 * math.log2(math.e)

_NT = (((1,), (1,)), ((), ()))


def _rms(x, g):
    return x * lax.rsqrt(jnp.mean(x * x, axis=-1, keepdims=True) + EPS) * g


def _sigmoid(x):
    return 1.0 / (1.0 + jnp.exp(-x))


def _const_spec(shape):
    nd = len(shape)
    return pl.BlockSpec(shape, lambda *_: (0,) * nd, pipeline_mode=pl.Buffered(1))


def _ffn_kernel(x_ref, pre_g_ref, wg_ref, wu_ref, wd_ref, post_g_ref, o_ref):
    x = x_ref[...]
    h = _rms(x, pre_g_ref[...]).astype(BF16)
    g = jnp.dot(h, wg_ref[...], preferred_element_type=F32)
    u = jnp.dot(h, wu_ref[...], preferred_element_type=F32)
    a = (g * _sigmoid(g) * u).astype(BF16)
    f = jnp.dot(a, wd_ref[...], preferred_element_type=F32)
    o_ref[...] = x + 0.5 * _rms(f, post_g_ref[...])


def _ffn_half(x, pre_g, wg, wu, wd, post_g, tm=256):
    t, d = x.shape
    dff = wg.shape[1]
    row = pl.BlockSpec((tm, d), lambda i: (i, 0))
    return pl.pallas_call(
        _ffn_kernel,
        out_shape=jax.ShapeDtypeStruct((t, d), F32),
        grid=(t // tm,),
        in_specs=[row, _const_spec((1, d)), _const_spec((d, dff)), _const_spec((d, dff)),
                  _const_spec((dff, d)), _const_spec((1, d))],
        out_specs=row,
        compiler_params=pltpu.CompilerParams(dimension_semantics=("parallel",),
                                             vmem_limit_bytes=VMEM_LIMIT),
        name="ffn_half",
    )(x, pre_g, wg, wu, wd, post_g)


def _rope_tables(pos):
    half = ROT // 2
    inv_freq = jnp.exp(-math.log(ROPE_THETA) * jnp.arange(half, dtype=F32) * (2.0 / ROT))
    ang = pos.astype(F32)[:, None] * inv_freq[None, :]
    cos, sin = jnp.cos(ang), jnp.sin(ang)
    t = pos.shape[0]
    ones = jnp.ones((t, HEAD_DIM - ROT), F32)
    c = jnp.concatenate([cos, cos, ones], axis=1)
    s_up = jnp.concatenate([-sin, jnp.zeros((t, HEAD_DIM - half), F32)], axis=1)
    s_dn = jnp.concatenate([jnp.zeros((t, half), F32), sin, jnp.zeros((t, HEAD_DIM - ROT), F32)], axis=1)
    dup = lambda a: jnp.concatenate([a, a], axis=1)
    return dup(c), dup(s_up), dup(s_dn)


def _proj_kernel(x_ref, g_ref, w_ref, c_ref, sup_ref, sdn_ref,
                 iq_o, q_o, kf_o, kb_o, vf_o, vb_o, ikf_o, ik2_o, iw_o):
    n = _rms(x_ref[...], g_ref[...]).astype(BF16)
    p = jnp.dot(n, w_ref[...], preferred_element_type=F32)
    c, s_up, s_dn = c_ref[...], sup_ref[...], sdn_ref[...]

    def rope(x):
        return x * c + pltpu.roll(x, LANES - ROT // 2, 1) * s_up + pltpu.roll(x, ROT // 2, 1) * s_dn

    a = ATTN_DIM
    for gidx in range(a // LANES):
        sl = slice(gidx * LANES, (gidx + 1) * LANES)
        iq_o[:, sl] = rope(p[:, sl]).astype(BF16)
        q_o[:, sl] = (rope(p[:, a + gidx * LANES:a + (gidx + 1) * LANES]) * Q_SCALE).astype(BF16)
        k = rope(p[:, 2 * a + gidx * LANES:2 * a + (gidx + 1) * LANES])
        kf_o[:, sl] = k
        kb_o[:, sl] = k.astype(BF16)
        v = p[:, 3 * a + gidx * LANES:3 * a + (gidx + 1) * LANES]
        vf_o[:, sl] = v
        vb_o[:, sl] = v.astype(BF16)
    ik = rope(p[:, 4 * a:4 * a + LANES])
    ikf_o[...] = ik[:, :IDX_DIM]
    ik2_o[...] = ik.astype(BF16)
    iw_o[...] = p[:, 4 * a + LANES:4 * a + 2 * LANES] * IDX_W_SCALE


def _mixer_proj(x, g, w2, tables, tm=256):
    t, d = x.shape
    ncol = w2.shape[1]
    row = lambda width: pl.BlockSpec((tm, width), lambda i: (i, 0))
    a = ATTN_DIM
    outs = [
        jax.ShapeDtypeStruct((t, a), BF16),
        jax.ShapeDtypeStruct((t, a), BF16),
        jax.ShapeDtypeStruct((t, a), F32),
        jax.ShapeDtypeStruct((t, a), BF16),
        jax.ShapeDtypeStruct((t, a), F32),
        jax.ShapeDtypeStruct((t, a), BF16),
        jax.ShapeDtypeStruct((t, IDX_DIM), F32),
        jax.ShapeDtypeStruct((t, LANES), BF16),
        jax.ShapeDtypeStruct((t, LANES), F32),
    ]
    return pl.pallas_call(
        _proj_kernel,
        out_shape=outs,
        grid=(t // tm,),
        in_specs=[row(d), _const_spec((1, d)), _const_spec((d, ncol)), row(LANES), row(LANES), row(LANES)],
        out_specs=[row(a), row(a), row(a), row(a), row(a), row(a), row(IDX_DIM), row(LANES), row(LANES)],
        compiler_params=pltpu.CompilerParams(dimension_semantics=("parallel",),
                                             vmem_limit_bytes=VMEM_LIMIT),
        name="mixer_proj",
    )(x, g, w2, *tables)


def _sortable(x):
    b = lax.bitcast_convert_type(x, I32)
    k = b ^ ((b >> 31) & 0x7FFFFFFF)
    return jnp.where(x == 0.0, 0, k)


def _head_lhs(x, h):
    grp, odd = divmod(h, 2)
    xg = x[:, grp * LANES:(grp + 1) * LANES].astype(F32)
    lane = lax.broadcasted_iota(I32, xg.shape, 1)
    keep = (lane >= HEAD_DIM) if odd else (lane < HEAD_DIM)
    return jnp.where(keep, xg, 0.0).astype(BF16)


def _stack_heads(x_ref, dst_ref, rows):
    x = x_ref[...]
    for h in range(N_HEADS):
        dst_ref[h * rows:(h + 1) * rows, :] = _head_lhs(x, h)


def _spread_weights(w, wrep_ref):
    for h in range(IDX_HEADS):
        wrep_ref[h] = jnp.broadcast_to(w[:, h:h + 1], wrep_ref.shape[1:])


def _index_scores(iqs_ref, wrep_ref, ikc):
    rows = wrep_ref.shape[1]
    d = lax.dot_general(iqs_ref[...], ikc, _NT, preferred_element_type=F32)
    groups = []
    for l in range(ikc.shape[0] // LANES):
        acc = None
        for h in range(IDX_HEADS):
            term = jnp.maximum(d[h * rows:(h + 1) * rows, l * LANES:(l + 1) * LANES], 0.0) * wrep_ref[h]
            acc = term if acc is None else acc + term
        groups.append(acc)
    return jnp.concatenate(groups, axis=1)


def _lane_fold(x):
    out = x[:, :LANES]
    for l in range(1, x.shape[1] // LANES):
        out = out + x[:, l * LANES:(l + 1) * LANES]
    return out


COUNT_ROWS = 32


def _count_tile(tile_ref, cand):
    rows = tile_ref.shape[0]
    step = min(COUNT_ROWS, rows)
    parts = []
    for r in range(0, rows, step):
        hit = jnp.where(tile_ref[r:r + step, :] >= cand[r:r + step], 1.0, 0.0)
        parts.append(_lane_fold(hit))
    return jnp.concatenate(parts, axis=0)


def _count_tile_hi(tile_ref, cand):
    rows = tile_ref.shape[0]
    step = min(COUNT_ROWS, rows)
    one, zero = jnp.int16(1), jnp.int16(0)
    parts = []
    for r in range(0, rows, step):
        c = cand[r:r + step]
        acc = None
        for l in range(tile_ref.shape[1] // LANES):
            hit = jnp.where(tile_ref[r:r + step, l * LANES:(l + 1) * LANES] >= c, one, zero)
            acc = hit if acc is None else acc + hit
        parts.append(acc)
    return jnp.concatenate(parts, axis=0)


def _key_hi(key):
    return (key >> 16).astype(I16)


def _select_threshold(count_ge, count_hi, n_adm, topk):
    c0 = count_hi(jnp.zeros(n_adm.shape, I32))
    c1 = count_ge(jnp.ones(n_adm.shape, I32))
    nonneg = c0 >= topk
    zero_tie = jnp.logical_and(nonneg, c1 < topk)

    def hi_bit(i, st):
        h, cnt = st
        cand = h + jnp.left_shift(jnp.int32(1), 15 - i)
        c = count_hi(cand)
        ok = c >= topk
        return jnp.where(ok, cand, h), jnp.where(ok, c, cnt)

    h, cnt = lax.fori_loop(1, 16, hi_bit, (jnp.where(nonneg, 0, -(2 ** 15)), jnp.where(nonneg, c0, n_adm)))

    def cond(st):
        i, _, cnt = st
        return jnp.logical_and(i < 32, jnp.max(jnp.where(zero_tie, 0.0, cnt)) > topk)

    def one_bit(i, t, cnt):
        cand = t + jnp.left_shift(jnp.int32(1), 31 - i)
        c = count_ge(cand)
        ok = c >= topk
        return jnp.where(ok, cand, t), jnp.where(ok, c, cnt)

    def step(st):
        i, t, cnt = st
        t, cnt = one_bit(i, t, cnt)
        t, cnt = one_bit(i + 1, t, cnt)
        return i + 2, t, cnt

    _, t, cnt = lax.while_loop(cond, step, (jnp.int32(16), jnp.left_shift(h, 16), cnt))
    return jnp.maximum(t, INT_MIN + 1), cnt


def _bias_plain(key, t):
    return jnp.where(key >= t, 0.0, NEG_BIG)


def _init_tri(tri_ref):
    @pl.when(pl.program_id(0) == 0)
    def _():
        r = lax.broadcasted_iota(I32, tri_ref.shape, 0)
        c = lax.broadcasted_iota(I32, tri_ref.shape, 1)
        tri_ref[...] = jnp.where(jnp.logical_or(r < c, c >= LANES), 1.0, 0.0).astype(BF16)


def _bias_ties(key, t, need, run, tri_ref):
    rows, sc = key.shape
    groups = sc // LANES
    eq = key == t
    eqb = jnp.where(eq, 1.0, 0.0).astype(BF16)
    stacked = jnp.concatenate([eqb[:, l * LANES:(l + 1) * LANES] for l in range(groups)], axis=0)
    counts = jnp.dot(stacked, tri_ref[...], preferred_element_type=F32)
    before = []
    for l in range(groups):
        blk = counts[l * rows:(l + 1) * rows]
        before.append(run + blk[:, :LANES])
        run = run + blk[:, LANES:]
    sel = (key > t) | (eq & (jnp.concatenate(before, axis=1) < need))
    return jnp.where(sel, 0.0, NEG_BIG), run


def _attend_chunk(qs_ref, bias, kc, vc, m_ref, acc_ref):
    pairs = N_HEADS // 2
    prow = qs_ref.shape[0] // pairs
    nrep = kc.shape[0] // LANES
    bias2 = jnp.concatenate([bias, bias], axis=0)
    ones = jnp.ones((vc.shape[0], LANES), BF16)
    scores = [lax.dot_general(qs_ref[g * prow:(g + 1) * prow, :], kc[:, g * LANES:(g + 1) * LANES], _NT,
                              preferred_element_type=F32) + bias2 for g in range(pairs)]
    probs, alphas = [], []
    for g in range(pairs):
        s = scores[g]
        m_prev = m_ref[g]
        m_next = jnp.maximum(m_prev, jnp.max(s, axis=1, keepdims=True))
        alphas.append(jnp.exp2(m_prev - m_next))
        probs.append(jnp.exp2(s - jnp.concatenate([m_next] * nrep, axis=1)).astype(BF16))
        m_ref[g] = m_next
    for g in range(pairs):
        vext = jnp.concatenate([vc[:, g * LANES:(g + 1) * LANES], ones], axis=1)
        pv = jnp.dot(probs[g], vext, preferred_element_type=F32)
        acc_ref[g] = jnp.concatenate([alphas[g], alphas[g]], axis=1) * acc_ref[g] + pv


def _init_attend(m_ref, acc_ref):
    m_ref[...] = jnp.full(m_ref.shape, NEG_BIG, F32)
    acc_ref[...] = jnp.zeros(acc_ref.shape, F32)


def _finish_attend(o_ref, acc_ref):
    rows = acc_ref.shape[1] // 2
    lane = lax.broadcasted_iota(I32, (rows, LANES), 1)
    for g in range(N_HEADS // 2):
        acc = acc_ref[g]
        o = acc[:, :LANES] / acc[:, LANES:]
        o_ref[:, g * LANES:(g + 1) * LANES] = jnp.where(lane < HEAD_DIM, o[:rows], o[rows:]).astype(o_ref.dtype)


def _prompt_attn_kernel(q_ref, iq_ref, iw_ref, kb_ref, vb_ref, ik2_ref, o_ref,
                        key_ref, khi_ref, iqs_ref, qs_ref, wrep_ref, m_ref, acc_ref, tri_ref, *, tq, sc, topk):
    j = pl.program_id(0)
    n_chunks = ((j + 1) * tq + sc - 1) // sc

    _init_tri(tri_ref)
    _stack_heads(iq_ref, iqs_ref, tq)
    _stack_heads(q_ref, qs_ref, tq)
    _spread_weights(iw_ref[...], wrep_ref)
    row = lax.broadcasted_iota(I32, (tq, 1), 0)
    limit = j * tq + (row // CHUNK + 1) * CHUNK

    def score_chunk(c, carry):
        off = pl.multiple_of(c * sc, sc)
        acc = _index_scores(iqs_ref, wrep_ref, ik2_ref[pl.ds(off, sc), :])
        sidx = off + lax.broadcasted_iota(I32, (tq, sc), 1)
        key = jnp.where(sidx < limit, _sortable(acc), INT_MIN)
        key_ref[c] = key
        khi_ref[c] = _key_hi(key)
        return carry
    lax.fori_loop(0, n_chunks, score_chunk, 0)

    def count_ge(cand):
        def body(c, acc):
            return acc + _count_tile(key_ref.at[c], cand)
        part = lax.fori_loop(0, n_chunks, body, jnp.zeros((tq, LANES), F32))
        return jnp.sum(part, axis=1, keepdims=True)

    def count_hi(cand):
        cand = jnp.broadcast_to(cand, (tq, LANES)).astype(I16)
        def body(c, acc):
            return acc + _count_tile_hi(khi_ref.at[c], cand)
        part = lax.fori_loop(0, n_chunks, body, jnp.zeros((tq, LANES), I16))
        return jnp.sum(part.astype(I32).astype(F32), axis=1, keepdims=True)

    t, n_ge = _select_threshold(count_ge, count_hi, limit.astype(F32), topk)
    has_ties = jnp.max(n_ge) > topk

    @pl.when(jnp.logical_not(has_ties))
    def _():
        def body(c, carry):
            key_ref[c] = lax.bitcast_convert_type(_bias_plain(key_ref[c], t), I32)
            return carry
        lax.fori_loop(0, n_chunks, body, 0)

    @pl.when(has_ties)
    def _():
        need = topk - count_ge(t + 1)
        def body(c, run):
            bias, run = _bias_ties(key_ref[c], t, need, run, tri_ref)
            key_ref[c] = lax.bitcast_convert_type(bias, I32)
            return run
        lax.fori_loop(0, n_chunks, body, jnp.zeros((tq, LANES), F32))

    _init_attend(m_ref, acc_ref)

    def attend(c, carry):
        off = pl.multiple_of(c * sc, sc)
        bias = lax.bitcast_convert_type(key_ref[c], F32)
        _attend_chunk(qs_ref, bias, kb_ref[pl.ds(off, sc), :], vb_ref[pl.ds(off, sc), :],
                      m_ref, acc_ref)
        return carry
    lax.fori_loop(0, n_chunks, attend, 0)
    _finish_attend(o_ref, acc_ref)


def _attn_scratch(rows):
    pairs = N_HEADS // 2
    return [
        pltpu.VMEM((N_HEADS * rows, LANES), BF16),
        pltpu.VMEM((N_HEADS * rows, LANES), BF16),
        pltpu.VMEM((IDX_HEADS, rows, LANES), F32),
        pltpu.VMEM((pairs, 2 * rows, LANES), F32),
        pltpu.VMEM((pairs, 2 * rows, 2 * LANES), F32),
        pltpu.VMEM((LANES, 2 * LANES), BF16),
    ]


def _prompt_attention(q, iq, iw, kb, vb, ik2, topk, tq=128, sc=512):
    s = kb.shape[0]
    row = lambda width: pl.BlockSpec((tq, width), lambda j: (j, 0))
    whole = pl.BlockSpec(memory_space=pltpu.VMEM)
    return pl.pallas_call(
        functools.partial(_prompt_attn_kernel, tq=tq, sc=sc, topk=topk),
        out_shape=jax.ShapeDtypeStruct((s, ATTN_DIM), BF16),
        grid=(s // tq,),
        in_specs=[row(ATTN_DIM), row(ATTN_DIM), row(LANES), whole, whole, whole],
        out_specs=row(ATTN_DIM),
        scratch_shapes=[pltpu.VMEM((s // sc, tq, sc), I32),
                        pltpu.VMEM((s // sc, tq, sc), I16)]
                       + _attn_scratch(tq),
        compiler_params=pltpu.CompilerParams(dimension_semantics=("arbitrary",),
                                             vmem_limit_bytes=VMEM_LIMIT),
        name="prompt_attention",
    )(q, iq, iw, kb, vb, ik2)


def _sample_attn_kernel(q_ref, iq_ref, iw_ref, kn_ref, vn_ref, ikn_ref, ck_ref, cv_ref, cik_ref, o_ref,
                        key_ref, keyn_ref, khi_ref, khin_ref, iqs_ref, qs_ref, wrep_ref, m_ref, acc_ref, tri_ref,
                        *, sc, topk):
    ts = q_ref.shape[0]
    past = ck_ref.shape[1]
    n_chunks = past // sc

    _init_tri(tri_ref)
    _stack_heads(iq_ref, iqs_ref, ts)
    _stack_heads(q_ref, qs_ref, ts)
    _spread_weights(iw_ref[...], wrep_ref)

    pad = lambda x: jnp.concatenate([x, jnp.zeros((LANES - ts, x.shape[1]), x.dtype)], axis=0)
    new_ok = lax.broadcasted_iota(I32, (ts, LANES), 1) < ts

    def cache_ik(c):
        ik = cik_ref[0, c * sc:(c + 1) * sc, :].astype(BF16)
        return jnp.concatenate([ik, ik], axis=1)

    for c in range(n_chunks):
        key = _sortable(_index_scores(iqs_ref, wrep_ref, cache_ik(c)))
        key_ref[c] = key
        khi_ref[c] = _key_hi(key)
    key = jnp.where(new_ok, _sortable(_index_scores(iqs_ref, wrep_ref, pad(ikn_ref[...]))), INT_MIN)
    keyn_ref[...] = key
    khin_ref[...] = _key_hi(key)

    def count_ge(cand):
        part = jnp.where(keyn_ref[...] >= cand, 1.0, 0.0)
        for c in range(n_chunks):
            part = part + _count_tile(key_ref.at[c], cand)
        return jnp.sum(part, axis=1, keepdims=True)

    def count_hi(cand):
        cand = jnp.broadcast_to(cand, (ts, LANES)).astype(I16)
        part = _count_tile_hi(khin_ref, cand)
        for c in range(n_chunks):
            part = part + _count_tile_hi(khi_ref.at[c], cand)
        return jnp.sum(part.astype(I32).astype(F32), axis=1, keepdims=True)

    t, n_ge = _select_threshold(count_ge, count_hi, jnp.full((ts, 1), past + ts, F32), topk)
    has_ties = jnp.max(n_ge) > topk

    @pl.when(jnp.logical_not(has_ties))
    def _():
        for c in range(n_chunks):
            key_ref[c] = lax.bitcast_convert_type(_bias_plain(key_ref[c], t), I32)
        keyn_ref[...] = lax.bitcast_convert_type(_bias_plain(keyn_ref[...], t), I32)

    @pl.when(has_ties)
    def _():
        need = topk - count_ge(t + 1)
        run = jnp.zeros((ts, LANES), F32)
        for c in range(n_chunks):
            bias, run = _bias_ties(key_ref[c], t, need, run, tri_ref)
            key_ref[c] = lax.bitcast_convert_type(bias, I32)
        bias, run = _bias_ties(keyn_ref[...], t, need, run, tri_ref)
        keyn_ref[...] = lax.bitcast_convert_type(bias, I32)

    _init_attend(m_ref, acc_ref)
    for c in range(n_chunks):
        _attend_chunk(qs_ref,lax.bitcast_convert_type(key_ref[c], F32),
                      ck_ref[0, c * sc:(c + 1) * sc, :].astype(BF16),
                      cv_ref[0, c * sc:(c + 1) * sc, :].astype(BF16), m_ref, acc_ref)
    _attend_chunk(qs_ref,lax.bitcast_convert_type(keyn_ref[...], F32),
                  pad(kn_ref[...]), pad(vn_ref[...]), m_ref, acc_ref)
    _finish_attend(o_ref, acc_ref)


def _sample_attention(q, iq, iw, kb, vb, ik2, cache_k, cache_v, cache_ik, ts, topk, sc=512):
    t = q.shape[0]
    nb, past = cache_k.shape[:2]
    row = lambda width: pl.BlockSpec((ts, width), lambda b: (b, 0))
    cache = lambda width: pl.BlockSpec((1, past, width), lambda b: (b, 0, 0))
    return pl.pallas_call(
        functools.partial(_sample_attn_kernel, sc=sc, topk=topk),
        out_shape=jax.ShapeDtypeStruct((t, ATTN_DIM), BF16),
        grid=(nb,),
        in_specs=[row(ATTN_DIM), row(ATTN_DIM), row(LANES), row(ATTN_DIM), row(ATTN_DIM), row(LANES),
                  cache(ATTN_DIM), cache(ATTN_DIM), cache(IDX_DIM)],
        out_specs=row(ATTN_DIM),
        scratch_shapes=([pltpu.VMEM((past // sc, ts, sc), I32), pltpu.VMEM((ts, LANES), I32),
                         pltpu.VMEM((past // sc, ts, sc), I16), pltpu.VMEM((ts, LANES), I16)]
                        + _attn_scratch(ts)),
        compiler_params=pltpu.CompilerParams(dimension_semantics=("arbitrary",),
                                             vmem_limit_bytes=VMEM_LIMIT),
        name="sample_attention",
    )(q, iq, iw, kb, vb, ik2, cache_k, cache_v, cache_ik)


HIST_ROW0 = 2


def _merge_kernel(x_ref, a_ref, hist_ref, g_ref, wglu_ref, wgate_ref, bgate_ref, wa_ref,
                  dww_ref, dwb_ref, lng_ref, lnb_ref, wc_ref, wo_ref, postg_ref,
                  y_ref, conv_o_ref, uext_ref, *, nseq, seqlen, carry):
    i = pl.program_id(0)
    x = x_ref[...]
    n = _rms(x, g_ref[...]).astype(BF16)
    glu = jnp.dot(n, wglu_ref[...], preferred_element_type=F32)
    u = glu[:, :CONV_CH] * _sigmoid(glu[:, CONV_CH:])

    new0 = HIST_ROW0 + HIST
    if carry:
        @pl.when(i == 0)
        def _():
            uext_ref[:, HIST_ROW0:new0, :] = hist_ref[...]

        @pl.when(i > 0)
        def _():
            uext_ref[:, HIST_ROW0:new0, :] = uext_ref[:, seqlen + HIST_ROW0:seqlen + new0, :]
    else:
        uext_ref[:, HIST_ROW0:new0, :] = hist_ref[...]
    uext_ref[:, new0:new0 + seqlen, :] = u.reshape(nseq, seqlen, CONV_CH)
    conv_o_ref[...] = uext_ref[:, seqlen + HIST_ROW0:seqlen + new0, :]

    dww = dww_ref[...]
    cdw = None
    for tap in range(CONV_WIDTH):
        term = uext_ref[:, HIST_ROW0 + tap:HIST_ROW0 + tap + seqlen, :] * dww[tap:tap + 1, :]
        cdw = term if cdw is None else cdw + term
    cdw = cdw.reshape(nseq * seqlen, CONV_CH) + dwb_ref[...]

    mu = jnp.mean(cdw, axis=-1, keepdims=True)
    xc = cdw - mu
    var = jnp.mean(xc * xc, axis=-1, keepdims=True)
    ln = xc * lax.rsqrt(var + EPS) * lng_ref[...] + lnb_ref[...]
    cact = (ln * _sigmoid(ln)).astype(BF16)
    c_proj = jnp.dot(cact, wc_ref[...], preferred_element_type=F32)
    a_proj = jnp.dot(a_ref[...], wa_ref[...], preferred_element_type=F32)

    d = x.shape[1]
    gate = _sigmoid(jnp.dot(n, wgate_ref[...], preferred_element_type=F32) + bgate_ref[...])
    merged = (gate[:, :d] * a_proj + gate[:, d:] * c_proj).astype(BF16)
    y = jnp.dot(merged, wo_ref[...], preferred_element_type=F32)
    y_ref[...] = x + _rms(y, postg_ref[...])


def _branch_merge(x, a, hist, g, wglu, wgate, bgate, wa, dww, dwb, lng, lnb, wc, wo, postg,
                  *, nseq, seqlen, carry):
    t, d = x.shape
    tm = nseq * seqlen
    row = lambda width: pl.BlockSpec((tm, width), lambda i: (i, 0))
    if carry:
        seq_spec = pl.BlockSpec((nseq, HIST, CONV_CH), lambda i: (0, 0, 0))
        n_state = nseq
    else:
        seq_spec = pl.BlockSpec((nseq, HIST, CONV_CH), lambda i: (i, 0, 0))
        n_state = t // seqlen
    consts = [g, wglu, wgate, bgate, wa, dww, dwb, lng, lnb, wc, wo, postg]
    return pl.pallas_call(
        functools.partial(_merge_kernel, nseq=nseq, seqlen=seqlen, carry=carry),
        out_shape=[jax.ShapeDtypeStruct((t, d), F32),
                   jax.ShapeDtypeStruct((n_state, HIST, CONV_CH), F32)],
        grid=(t // tm,),
        in_specs=[row(d), row(ATTN_DIM), seq_spec] + [_const_spec(c.shape) for c in consts],
        out_specs=[row(d), seq_spec],
        scratch_shapes=[pltpu.VMEM((nseq, HIST_ROW0 + HIST + seqlen, CONV_CH), F32)],
        compiler_params=pltpu.CompilerParams(dimension_semantics=("arbitrary",),
                                             vmem_limit_bytes=VMEM_LIMIT),
        name="branch_merge",
    )(x, a, hist, *consts)


def kernel(x_prompt, x_sample, cache_k, cache_v, cache_idx_k, state_conv,
           ffn1_pre_g, ffn1_w_gate, ffn1_w_up, ffn1_w_down, ffn1_post_g,
           mix_pre_g, w_in, b_gate, w_attn_o, conv_dw_w, conv_dw_b, conv_ln_g, conv_ln_b,
           w_conv_o, w_out, mix_post_g,
           ffn2_pre_g, ffn2_w_gate, ffn2_w_up, ffn2_w_down, ffn2_post_g):
    bp, lp, d = x_prompt.shape
    db, ts, _ = x_sample.shape
    depth, _, past = cache_k.shape[:3]
    assert bp == 1 and depth == 1
    topk_p = min(TOPK_MAX, lp // 4)
    topk_s = min(TOPK_MAX, (past + ts) // 4)

    vec = lambda a: a.reshape(1, -1)
    bf = lambda a: a.astype(BF16)
    ffn1 = (vec(ffn1_pre_g[0]), bf(ffn1_w_gate[0]), bf(ffn1_w_up[0]), bf(ffn1_w_down[0]), vec(ffn1_post_g[0]))
    ffn2 = (vec(ffn2_pre_g[0]), bf(ffn2_w_gate[0]), bf(ffn2_w_up[0]), bf(ffn2_w_down[0]), vec(ffn2_post_g[0]))

    widths = (IDX_HEADS * IDX_DIM, IDX_HEADS, IDX_DIM, ATTN_DIM, ATTN_DIM, ATTN_DIM, 2 * CONV_CH, 2 * d)
    offs = [0]
    for wd_ in widths:
        offs.append(offs[-1] + wd_)
    wi = w_in[0]
    col = lambda k: wi[:, offs[k]:offs[k + 1]]
    w_iw = jnp.pad(col(1), ((0, 0), (0, LANES - IDX_HEADS)))
    w2 = bf(jnp.concatenate([col(0), col(3), col(4), col(5), col(2), col(2), w_iw], axis=1))
    merge_w = (vec(mix_pre_g[0]), bf(col(6)), bf(col(7)), vec(b_gate[0]), bf(w_attn_o[0]),
               conv_dw_w[0], vec(conv_dw_b[0]), vec(conv_ln_g[0]), vec(conv_ln_b[0]),
               bf(w_conv_o[0]), bf(w_out[0]), vec(mix_post_g[0]))

    pos_p = jnp.arange(lp, dtype=I32)
    pos_s = jnp.tile(past + jnp.arange(ts, dtype=I32), db)

    hp = _ffn_half(x_prompt.reshape(lp, d), *ffn1)
    iq, q, kf, kb, vf, vb, ikf, ik2, iw = _mixer_proj(hp, vec(mix_pre_g[0]), w2, _rope_tables(pos_p))
    a = _prompt_attention(q, iq, iw, kb, vb, ik2, topk_p)
    hp, conv_p = _branch_merge(hp, a, jnp.zeros((1, HIST, CONV_CH), F32), *merge_w,
                               nseq=1, seqlen=256, carry=True)
    hp = _ffn_half(hp, *ffn2)
    outs_p = (kf.reshape(1, 1, lp, N_HEADS, HEAD_DIM), vf.reshape(1, 1, lp, N_HEADS, HEAD_DIM),
              ikf.reshape(1, 1, lp, IDX_DIM), conv_p.reshape(1, 1, HIST, CONV_CH))

    n_s = db * ts
    hs = _ffn_half(x_sample.reshape(n_s, d), *ffn1)
    iq, q, kf, kb, vf, vb, ikf, ik2, iw = _mixer_proj(hs, vec(mix_pre_g[0]), w2, _rope_tables(pos_s))
    a = _sample_attention(q, iq, iw, kb, vb, ik2, cache_k[0].reshape(db, past, ATTN_DIM),
                          cache_v[0].reshape(db, past, ATTN_DIM), cache_idx_k[0], ts, topk_s)
    hs, conv_s = _branch_merge(hs, a, state_conv[0], *merge_w, nseq=8, seqlen=ts, carry=False)
    hs = _ffn_half(hs, *ffn2)
    outs_s = (kf.reshape(1, db, ts, N_HEADS, HEAD_DIM), vf.reshape(1, db, ts, N_HEADS, HEAD_DIM),
              ikf.reshape(1, db, ts, IDX_DIM), conv_s.reshape(1, db, HIST, CONV_CH))

    return (hp.reshape(1, lp, d), hs.reshape(db, ts, d)) + outs_p + outs_s
```

```python
import functools
import math

import jax
import jax.numpy as jnp
from jax import lax
from jax.experimental import pallas as pl
from jax.experimental.pallas import tpu as pltpu

F32 = jnp.float32
BF16 = jnp.bfloat16
I32 = jnp.int32

CHUNK = 64
N_HEADS = 8
HEAD_DIM = 64
ATTN_DIM = N_HEADS * HEAD_DIM
IDX_HEADS = 8
IDX_DIM = 64
IDX_W_SCALE = (IDX_HEADS * IDX_DIM) ** -0.5
TOPK_MAX = 256
ROPE_THETA = 500000.0
ROT = HEAD_DIM // 4
CONV_CH = 512
CONV_WIDTH = 31
HIST = CONV_WIDTH - 1
EPS = 1e-6

LANES = 128
VMEM_LIMIT = 56 * 1024 * 1024
INT_MIN = -(2 ** 31)
NEG_BIG = -1e30
Q_SCALE = HEAD_DIM ** -0.5 * math.log2(math.e)

_NT = (((1,), (1,)), ((), ()))


def _rms(x, g):
    return x * lax.rsqrt(jnp.mean(x * x, axis=-1, keepdims=True) + EPS) * g


def _sigmoid(x):
    return 1.0 / (1.0 + jnp.exp(-x))


def _const_spec(shape):
    nd = len(shape)
    return pl.BlockSpec(shape, lambda *_: (0,) * nd, pipeline_mode=pl.Buffered(1))


def _ffn_kernel(x_ref, pre_g_ref, wg_ref, wu_ref, wd_ref, post_g_ref, o_ref):
    x = x_ref[...]
    h = _rms(x, pre_g_ref[...]).astype(BF16)
    g = jnp.dot(h, wg_ref[...], preferred_element_type=F32)
    u = jnp.dot(h, wu_ref[...], preferred_element_type=F32)
    a = (g * _sigmoid(g) * u).astype(BF16)
    f = jnp.dot(a, wd_ref[...], preferred_element_type=F32)
    o_ref[...] = x + 0.5 * _rms(f, post_g_ref[...])


def _ffn_half(x, pre_g, wg, wu, wd, post_g, tm=512):
    t, d = x.shape
    dff = wg.shape[1]
    row = pl.BlockSpec((tm, d), lambda i: (i, 0))
    return pl.pallas_call(
        _ffn_kernel,
        out_shape=jax.ShapeDtypeStruct((t, d), F32),
        grid=(t // tm,),
        in_specs=[row, _const_spec((1, d)), _const_spec((d, dff)), _const_spec((d, dff)),
                  _const_spec((dff, d)), _const_spec((1, d))],
        out_specs=row,
        compiler_params=pltpu.CompilerParams(dimension_semantics=("parallel",),
                                             vmem_limit_bytes=VMEM_LIMIT),
        name="ffn_half",
    )(x, pre_g, wg, wu, wd, post_g)


def _rope_angles(pos):
    half = ROT // 2
    inv_freq = jnp.exp(-math.log(ROPE_THETA) * jnp.arange(half, dtype=F32) * (2.0 / ROT))
    ang = pos.astype(F32)[:, None] * inv_freq[None, :]
    return jnp.pad(jnp.concatenate([jnp.cos(ang), jnp.sin(ang)], axis=1), ((0, 0), (0, LANES - ROT)))


def _rope_tables(cs):
    half = ROT // 2
    lane = lax.broadcasted_iota(I32, cs.shape, 1)
    both = lambda a: a + pltpu.roll(a, HEAD_DIM, 1)
    cos_lo = jnp.where(lane < half, cs, 0.0)
    sin_hi = jnp.where(jnp.logical_and(lane >= half, lane < ROT), cs, 0.0)
    c = both(cos_lo + pltpu.roll(cos_lo, half, 1))
    c = jnp.where(lane % HEAD_DIM < ROT, c, 1.0)
    s_dn = both(sin_hi)
    s_up = -both(pltpu.roll(sin_hi, LANES - half, 1))
    return c, s_up, s_dn


def _proj_kernel(x_ref, g_ref, w_ref, cs_ref,
                 iq_o, q_o, kf_o, kb_o, vf_o, vb_o, ikf_o, ik2_o, iw_o):
    n = _rms(x_ref[...], g_ref[...]).astype(BF16)
    p = jnp.dot(n, w_ref[...], preferred_element_type=F32)
    c, s_up, s_dn = _rope_tables(cs_ref[...])

    def rope(x):
        return x * c + pltpu.roll(x, LANES - ROT // 2, 1) * s_up + pltpu.roll(x, ROT // 2, 1) * s_dn

    a = ATTN_DIM
    for gidx in range(a // LANES):
        sl = slice(gidx * LANES, (gidx + 1) * LANES)
        iq_o[:, sl] = rope(p[:, sl]).astype(BF16)
        q_o[:, sl] = (rope(p[:, a + gidx * LANES:a + (gidx + 1) * LANES]) * Q_SCALE).astype(BF16)
        k = rope(p[:, 2 * a + gidx * LANES:2 * a + (gidx + 1) * LANES])
        kf_o[:, sl] = k
        kb_o[:, sl] = k.astype(BF16)
        v = p[:, 3 * a + gidx * LANES:3 * a + (gidx + 1) * LANES]
        vf_o[:, sl] = v
        vb_o[:, sl] = v.astype(BF16)
    ik = rope(p[:, 4 * a:4 * a + LANES])
    ikf_o[...] = ik[:, :IDX_DIM]
    ik2_o[...] = ik.astype(BF16)
    iw_o[...] = p[:, 4 * a + LANES:4 * a + 2 * LANES] * IDX_W_SCALE


def _mixer_proj(x, g, w2, angles, tm=512):
    t, d = x.shape
    ncol = w2.shape[1]
    row = lambda width: pl.BlockSpec((tm, width), lambda i: (i, 0))
    a = ATTN_DIM
    outs = [
        jax.ShapeDtypeStruct((t, a), BF16),
        jax.ShapeDtypeStruct((t, a), BF16),
        jax.ShapeDtypeStruct((t, a), F32),
        jax.ShapeDtypeStruct((t, a), BF16),
        jax.ShapeDtypeStruct((t, a), F32),
        jax.ShapeDtypeStruct((t, a), BF16),
        jax.ShapeDtypeStruct((t, IDX_DIM), F32),
        jax.ShapeDtypeStruct((t, LANES), BF16),
        jax.ShapeDtypeStruct((t, LANES), F32),
    ]
    return pl.pallas_call(
        _proj_kernel,
        out_shape=outs,
        grid=(t // tm,),
        in_specs=[row(d), _const_spec((1, d)), _const_spec((d, ncol)), row(LANES)],
        out_specs=[row(a), row(a), row(a), row(a), row(a), row(a), row(IDX_DIM), row(LANES), row(LANES)],
        compiler_params=pltpu.CompilerParams(dimension_semantics=("parallel",),
                                             vmem_limit_bytes=VMEM_LIMIT),
        name="mixer_proj",
    )(x, g, w2, angles)


def _sortable(x):
    b = lax.bitcast_convert_type(x, I32)
    k = b ^ ((b >> 31) & 0x7FFFFFFF)
    return jnp.where(x == 0.0, 0, k)


def _head_lhs(x, h):
    grp, odd = divmod(h, 2)
    xg = x[:, grp * LANES:(grp + 1) * LANES].astype(F32)
    lane = lax.broadcasted_iota(I32, xg.shape, 1)
    keep = (lane >= HEAD_DIM) if odd else (lane < HEAD_DIM)
    return jnp.where(keep, xg, 0.0).astype(BF16)


def _stack_heads(x_ref, dst_ref, rows):
    x = x_ref[...]
    for h in range(N_HEADS):
        dst_ref[h * rows:(h + 1) * rows, :] = _head_lhs(x, h)


def _spread_weights(w, wrep_ref):
    for h in range(IDX_HEADS):
        wrep_ref[h] = jnp.broadcast_to(w[:, h:h + 1], wrep_ref.shape[1:])


def _index_scores(iqs_ref, wrep_ref, ikc):
    rows = wrep_ref.shape[1]
    d = lax.dot_general(iqs_ref[...], ikc, _NT, preferred_element_type=F32)
    groups = []
    for l in range(ikc.shape[0] // LANES):
        acc = None
        for h in range(IDX_HEADS):
            term = jnp.maximum(d[h * rows:(h + 1) * rows, l * LANES:(l + 1) * LANES], 0.0) * wrep_ref[h]
            acc = term if acc is None else acc + term
        groups.append(acc)
    return jnp.concatenate(groups, axis=1)


def _lane_fold(x):
    out = x[:, :LANES]
    for l in range(1, x.shape[1] // LANES):
        out = out + x[:, l * LANES:(l + 1) * LANES]
    return out


COUNT_ROWS = 32


def _count_tile(tile_ref, cand):
    rows = tile_ref.shape[0]
    step = min(COUNT_ROWS, rows)
    parts = []
    for r in range(0, rows, step):
        hit = jnp.where(tile_ref[r:r + step, :] >= cand[r:r + step], 1.0, 0.0)
        parts.append(_lane_fold(hit))
    return jnp.concatenate(parts, axis=0)


def _select_threshold(count_ge, n_adm, topk):
    c0 = count_ge(jnp.zeros(n_adm.shape, I32))
    c1 = count_ge(jnp.ones(n_adm.shape, I32))
    nonneg = c0 >= topk
    zero_tie = jnp.logical_and(nonneg, c1 < topk)

    def cond(st):
        i, _, cnt = st
        return jnp.logical_and(i < 32, jnp.max(jnp.where(zero_tie, 0.0, cnt)) > topk)

    def one_bit(i, t, cnt):
        cand = t + jnp.left_shift(jnp.int32(1), 31 - i)
        c = count_ge(cand)
        ok = c >= topk
        return jnp.where(ok, cand, t), jnp.where(ok, c, cnt)

    def step(st):
        i, t, cnt = st
        t, cnt = one_bit(i, t, cnt)
        t, cnt = one_bit(i + 1, t, cnt)
        return i + 2, t, cnt

    t, cnt = one_bit(1, jnp.where(nonneg, 0, INT_MIN), jnp.where(nonneg, c0, n_adm))
    _, t, cnt = lax.while_loop(cond, step, (jnp.int32(2), t, cnt))
    return jnp.maximum(t, INT_MIN + 1), cnt


def _bias_plain(key, t):
    return jnp.where(key >= t, 0.0, NEG_BIG)


def _init_tri(tri_ref):
    @pl.when(pl.program_id(0) == 0)
    def _():
        r = lax.broadcasted_iota(I32, tri_ref.shape, 0)
        c = lax.broadcasted_iota(I32, tri_ref.shape, 1)
        tri_ref[...] = jnp.where(jnp.logical_or(r < c, c >= LANES), 1.0, 0.0).astype(BF16)


def _bias_ties(key, t, need, run, tri_ref):
    rows, sc = key.shape
    groups = sc // LANES
    eq = key == t
    eqb = jnp.where(eq, 1.0, 0.0).astype(BF16)
    stacked = jnp.concatenate([eqb[:, l * LANES:(l + 1) * LANES] for l in range(groups)], axis=0)
    counts = jnp.dot(stacked, tri_ref[...], preferred_element_type=F32)
    before = []
    for l in range(groups):
        blk = counts[l * rows:(l + 1) * rows]
        before.append(run + blk[:, :LANES])
        run = run + blk[:, LANES:]
    sel = (key > t) | (eq & (jnp.concatenate(before, axis=1) < need))
    return jnp.where(sel, 0.0, NEG_BIG), run


def _pair_scores(qs_ref, kc):
    pairs = N_HEADS // 2
    prow = qs_ref.shape[0] // pairs
    return [lax.dot_general(qs_ref[g * prow:(g + 1) * prow, :], kc[:, g * LANES:(g + 1) * LANES], _NT,
                            preferred_element_type=F32) for g in range(pairs)]


def _softmax_pv(scores, bias, vc, m_ref, acc_ref):
    pairs = N_HEADS // 2
    nrep = vc.shape[0] // LANES
    bias2 = jnp.concatenate([bias, bias], axis=0)
    ones = jnp.ones((vc.shape[0], LANES), BF16)
    probs, alphas = [], []
    for g in range(pairs):
        s = scores[g] + bias2
        m_prev = m_ref[g]
        m_next = jnp.maximum(m_prev, jnp.max(s, axis=1, keepdims=True))
        alphas.append(jnp.exp2(m_prev - m_next))
        probs.append(jnp.exp2(s - jnp.concatenate([m_next] * nrep, axis=1)).astype(BF16))
        m_ref[g] = m_next
    for g in range(pairs):
        vext = jnp.concatenate([vc[:, g * LANES:(g + 1) * LANES], ones], axis=1)
        pv = jnp.dot(probs[g], vext, preferred_element_type=F32)
        acc_ref[g] = jnp.concatenate([alphas[g], alphas[g]], axis=1) * acc_ref[g] + pv


def _attend_chunk(qs_ref, bias, kc, vc, m_ref, acc_ref):
    _softmax_pv(_pair_scores(qs_ref, kc), bias, vc, m_ref, acc_ref)


def _init_attend(m_ref, acc_ref):
    m_ref[...] = jnp.full(m_ref.shape, NEG_BIG, F32)
    acc_ref[...] = jnp.zeros(acc_ref.shape, F32)


def _finish_attend(o_ref, acc_ref):
    rows = acc_ref.shape[1] // 2
    lane = lax.broadcasted_iota(I32, (rows, LANES), 1)
    for g in range(N_HEADS // 2):
        acc = acc_ref[g]
        o = acc[:, :LANES] / acc[:, LANES:]
        o_ref[:, g * LANES:(g + 1) * LANES] = jnp.where(lane < HEAD_DIM, o[:rows], o[rows:]).astype(o_ref.dtype)


def _prompt_attn_kernel(q_ref, iq_ref, iw_ref, kb_ref, vb_ref, ik2_ref, o_ref,
                        key_ref, iqs_ref, qs_ref, wrep_ref, m_ref, acc_ref, tri_ref, *, tq, sc, topk):
    j = pl.program_id(0)
    n_chunks = ((j + 1) * tq + sc - 1) // sc

    _init_tri(tri_ref)
    _stack_heads(iq_ref, iqs_ref, tq)
    _stack_heads(q_ref, qs_ref, tq)
    _spread_weights(iw_ref[...], wrep_ref)
    row = lax.broadcasted_iota(I32, (tq, 1), 0)
    limit = j * tq + (row // CHUNK + 1) * CHUNK

    def score_chunk(c, carry):
        off = pl.multiple_of(c * sc, sc)
        acc = _index_scores(iqs_ref, wrep_ref, ik2_ref[pl.ds(off, sc), :])
        sidx = off + lax.broadcasted_iota(I32, (tq, sc), 1)
        key_ref[c] = jnp.where(sidx < limit, _sortable(acc), INT_MIN)
        return carry
    lax.fori_loop(0, n_chunks, score_chunk, 0)

    def count_ge(cand):
        def body(c, acc):
            return acc + _count_tile(key_ref.at[c], cand)
        part = lax.fori_loop(0, n_chunks, body, jnp.zeros((tq, LANES), F32))
        return jnp.sum(part, axis=1, keepdims=True)

    t, n_ge = _select_threshold(count_ge, limit.astype(F32), topk)
    has_ties = jnp.max(n_ge) > topk

    @pl.when(jnp.logical_not(has_ties))
    def _():
        def body(c, carry):
            key_ref[c] = lax.bitcast_convert_type(_bias_plain(key_ref[c], t), I32)
            return carry
        lax.fori_loop(0, n_chunks, body, 0)

    @pl.when(has_ties)
    def _():
        need = topk - count_ge(t + 1)
        def body(c, run):
            bias, run = _bias_ties(key_ref[c], t, need, run, tri_ref)
            key_ref[c] = lax.bitcast_convert_type(bias, I32)
            return run
        lax.fori_loop(0, n_chunks, body, jnp.zeros((tq, LANES), F32))

    _init_attend(m_ref, acc_ref)

    def attend(c, carry):
        off = pl.multiple_of(c * sc, sc)
        bias = lax.bitcast_convert_type(key_ref[c], F32)
        _attend_chunk(qs_ref, bias, kb_ref[pl.ds(off, sc), :], vb_ref[pl.ds(off, sc), :],
                      m_ref, acc_ref)
        return carry
    lax.fori_loop(0, n_chunks, attend, 0)
    _finish_attend(o_ref, acc_ref)


def _attn_scratch(rows):
    pairs = N_HEADS // 2
    return [
        pltpu.VMEM((N_HEADS * rows, LANES), BF16),
        pltpu.VMEM((N_HEADS * rows, LANES), BF16),
        pltpu.VMEM((IDX_HEADS, rows, LANES), F32),
        pltpu.VMEM((pairs, 2 * rows, LANES), F32),
        pltpu.VMEM((pairs, 2 * rows, 2 * LANES), F32),
        pltpu.VMEM((LANES, 2 * LANES), BF16),
    ]


def _prompt_attention(q, iq, iw, kb, vb, ik2, topk, tq=128, sc=512):
    s = kb.shape[0]
    row = lambda width: pl.BlockSpec((tq, width), lambda j: (j, 0))
    whole = pl.BlockSpec(memory_space=pltpu.VMEM)
    return pl.pallas_call(
        functools.partial(_prompt_attn_kernel, tq=tq, sc=sc, topk=topk),
        out_shape=jax.ShapeDtypeStruct((s, ATTN_DIM), BF16),
        grid=(s // tq,),
        in_specs=[row(ATTN_DIM), row(ATTN_DIM), row(LANES), whole, whole, whole],
        out_specs=row(ATTN_DIM),
        scratch_shapes=[pltpu.VMEM((s // sc, tq, sc), I32)] + _attn_scratch(tq),
        compiler_params=pltpu.CompilerParams(dimension_semantics=("arbitrary",),
                                             vmem_limit_bytes=VMEM_LIMIT),
        name="prompt_attention",
    )(q, iq, iw, kb, vb, ik2)


def _sample_attn_kernel(q_ref, iq_ref, iw_ref, kn_ref, vn_ref, ikn_ref, ck_ref, cv_ref, cik_ref, o_ref,
                        key_ref, keyn_ref, iqs_ref, qs_ref, wrep_ref, m_ref, acc_ref, tri_ref, *, sc, topk):
    ts = q_ref.shape[0]
    past = ck_ref.shape[1]
    n_chunks = past // sc

    _init_tri(tri_ref)
    _stack_heads(iq_ref, iqs_ref, ts)
    _stack_heads(q_ref, qs_ref, ts)
    _spread_weights(iw_ref[...], wrep_ref)

    pad = lambda x: jnp.concatenate([x, jnp.zeros((LANES - ts, x.shape[1]), x.dtype)], axis=0)
    new_ok = lax.broadcasted_iota(I32, (ts, LANES), 1) < ts

    def cache_ik(c):
        ik = cik_ref[0, c * sc:(c + 1) * sc, :].astype(BF16)
        return jnp.concatenate([ik, ik], axis=1)

    for c in range(n_chunks):
        key_ref[c] = _sortable(_index_scores(iqs_ref, wrep_ref, cache_ik(c)))
    keyn_ref[...] = jnp.where(new_ok, _sortable(_index_scores(iqs_ref, wrep_ref, pad(ikn_ref[...]))), INT_MIN)

    def count_ge(cand):
        part = jnp.where(keyn_ref[...] >= cand, 1.0, 0.0)
        for c in range(n_chunks):
            part = part + _count_tile(key_ref.at[c], cand)
        return jnp.sum(part, axis=1, keepdims=True)

    t, n_ge = _select_threshold(count_ge, jnp.full((ts, 1), past + ts, F32), topk)
    has_ties = jnp.max(n_ge) > topk

    @pl.when(jnp.logical_not(has_ties))
    def _():
        for c in range(n_chunks):
            key_ref[c] = lax.bitcast_convert_type(_bias_plain(key_ref[c], t), I32)
        keyn_ref[...] = lax.bitcast_convert_type(_bias_plain(keyn_ref[...], t), I32)

    @pl.when(has_ties)
    def _():
        need = topk - count_ge(t + 1)
        run = jnp.zeros((ts, LANES), F32)
        for c in range(n_chunks):
            bias, run = _bias_ties(key_ref[c], t, need, run, tri_ref)
            key_ref[c] = lax.bitcast_convert_type(bias, I32)
        bias, run = _bias_ties(keyn_ref[...], t, need, run, tri_ref)
        keyn_ref[...] = lax.bitcast_convert_type(bias, I32)

    _init_attend(m_ref, acc_ref)
    for c in range(n_chunks):
        _attend_chunk(qs_ref,lax.bitcast_convert_type(key_ref[c], F32),
                      ck_ref[0, c * sc:(c + 1) * sc, :].astype(BF16),
                      cv_ref[0, c * sc:(c + 1) * sc, :].astype(BF16), m_ref, acc_ref)
    _attend_chunk(qs_ref,lax.bitcast_convert_type(keyn_ref[...], F32),
                  pad(kn_ref[...]), pad(vn_ref[...]), m_ref, acc_ref)
    _finish_attend(o_ref, acc_ref)


def _sample_attention(q, iq, iw, kb, vb, ik2, cache_k, cache_v, cache_ik, ts, topk, sc=512):
    t = q.shape[0]
    nb, past = cache_k.shape[:2]
    row = lambda width: pl.BlockSpec((ts, width), lambda b: (b, 0))
    cache = lambda width: pl.BlockSpec((1, past, width), lambda b: (b, 0, 0))
    return pl.pallas_call(
        functools.partial(_sample_attn_kernel, sc=sc, topk=topk),
        out_shape=jax.ShapeDtypeStruct((t, ATTN_DIM), BF16),
        grid=(nb,),
        in_specs=[row(ATTN_DIM), row(ATTN_DIM), row(LANES), row(ATTN_DIM), row(ATTN_DIM), row(LANES),
                  cache(ATTN_DIM), cache(ATTN_DIM), cache(IDX_DIM)],
        out_specs=row(ATTN_DIM),
        scratch_shapes=([pltpu.VMEM((past // sc, ts, sc), I32), pltpu.VMEM((ts, LANES), I32)]
                        + _attn_scratch(ts)),
        compiler_params=pltpu.CompilerParams(dimension_semantics=("arbitrary",),
                                             vmem_limit_bytes=VMEM_LIMIT),
        name="sample_attention",
    )(q, iq, iw, kb, vb, ik2, cache_k, cache_v, cache_ik)


HIST_ROW0 = 2


def _merge_kernel(x_ref, a_ref, hist_ref, g_ref, wglu_ref, wgate_ref, bgate_ref, wa_ref,
                  dww_ref, dwb_ref, lng_ref, lnb_ref, wc_ref, wo_ref, postg_ref,
                  y_ref, conv_o_ref, uext_ref, *, nseq, seqlen, carry):
    i = pl.program_id(0)
    x = x_ref[...]
    n = _rms(x, g_ref[...]).astype(BF16)
    glu = jnp.dot(n, wglu_ref[...], preferred_element_type=F32)
    u = glu[:, :CONV_CH] * _sigmoid(glu[:, CONV_CH:])

    new0 = HIST_ROW0 + HIST
    if carry:
        @pl.when(i == 0)
        def _():
            uext_ref[:, HIST_ROW0:new0, :] = hist_ref[...]

        @pl.when(i > 0)
        def _():
            uext_ref[:, HIST_ROW0:new0, :] = uext_ref[:, seqlen + HIST_ROW0:seqlen + new0, :]
    else:
        uext_ref[:, HIST_ROW0:new0, :] = hist_ref[...]
    uext_ref[:, new0:new0 + seqlen, :] = u.reshape(nseq, seqlen, CONV_CH)
    conv_o_ref[...] = uext_ref[:, seqlen + HIST_ROW0:seqlen + new0, :]

    dww = dww_ref[...]
    cdw = None
    for tap in range(CONV_WIDTH):
        term = uext_ref[:, HIST_ROW0 + tap:HIST_ROW0 + tap + seqlen, :] * dww[tap:tap + 1, :]
        cdw = term if cdw is None else cdw + term
    cdw = cdw.reshape(nseq * seqlen, CONV_CH) + dwb_ref[...]

    mu = jnp.mean(cdw, axis=-1, keepdims=True)
    xc = cdw - mu
    var = jnp.mean(xc * xc, axis=-1, keepdims=True)
    ln = xc * lax.rsqrt(var + EPS) * lng_ref[...] + lnb_ref[...]
    cact = (ln * _sigmoid(ln)).astype(BF16)
    c_proj = jnp.dot(cact, wc_ref[...], preferred_element_type=F32)
    a_proj = jnp.dot(a_ref[...], wa_ref[...], preferred_element_type=F32)

    d = x.shape[1]
    gate = _sigmoid(jnp.dot(n, wgate_ref[...], preferred_element_type=F32) + bgate_ref[...])
    merged = (gate[:, :d] * a_proj + gate[:, d:] * c_proj).astype(BF16)
    y = jnp.dot(merged, wo_ref[...], preferred_element_type=F32)
    y_ref[...] = x + _rms(y, postg_ref[...])


def _branch_merge(x, a, hist, g, wglu, wgate, bgate, wa, dww, dwb, lng, lnb, wc, wo, postg,
                  *, nseq, seqlen, carry):
    t, d = x.shape
    tm = nseq * seqlen
    row = lambda width: pl.BlockSpec((tm, width), lambda i: (i, 0))
    if carry:
        seq_spec = pl.BlockSpec((nseq, HIST, CONV_CH), lambda i: (0, 0, 0))
        n_state = nseq
    else:
        seq_spec = pl.BlockSpec((nseq, HIST, CONV_CH), lambda i: (i, 0, 0))
        n_state = t // seqlen
    consts = [g, wglu, wgate, bgate, wa, dww, dwb, lng, lnb, wc, wo, postg]
    return pl.pallas_call(
        functools.partial(_merge_kernel, nseq=nseq, seqlen=seqlen, carry=carry),
        out_shape=[jax.ShapeDtypeStruct((t, d), F32),
                   jax.ShapeDtypeStruct((n_state, HIST, CONV_CH), F32)],
        grid=(t // tm,),
        in_specs=[row(d), row(ATTN_DIM), seq_spec] + [_const_spec(c.shape) for c in consts],
        out_specs=[row(d), seq_spec],
        scratch_shapes=[pltpu.VMEM((nseq, HIST_ROW0 + HIST + seqlen, CONV_CH), F32)],
        compiler_params=pltpu.CompilerParams(dimension_semantics=("arbitrary",),
                                             vmem_limit_bytes=VMEM_LIMIT),
        name="branch_merge",
    )(x, a, hist, *consts)


def kernel(x_prompt, x_sample, cache_k, cache_v, cache_idx_k, state_conv,
           ffn1_pre_g, ffn1_w_gate, ffn1_w_up, ffn1_w_down, ffn1_post_g,
           mix_pre_g, w_in, b_gate, w_attn_o, conv_dw_w, conv_dw_b, conv_ln_g, conv_ln_b,
           w_conv_o, w_out, mix_post_g,
           ffn2_pre_g, ffn2_w_gate, ffn2_w_up, ffn2_w_down, ffn2_post_g):
    bp, lp, d = x_prompt.shape
    db, ts, _ = x_sample.shape
    depth, _, past = cache_k.shape[:3]
    assert bp == 1 and depth == 1
    topk_p = min(TOPK_MAX, lp // 4)
    topk_s = min(TOPK_MAX, (past + ts) // 4)

    vec = lambda a: a.reshape(1, -1)
    bf = lambda a: a.astype(BF16)
    ffn1 = (vec(ffn1_pre_g[0]), bf(ffn1_w_gate[0]), bf(ffn1_w_up[0]), bf(ffn1_w_down[0]), vec(ffn1_post_g[0]))
    ffn2 = (vec(ffn2_pre_g[0]), bf(ffn2_w_gate[0]), bf(ffn2_w_up[0]), bf(ffn2_w_down[0]), vec(ffn2_post_g[0]))

    widths = (IDX_HEADS * IDX_DIM, IDX_HEADS, IDX_DIM, ATTN_DIM, ATTN_DIM, ATTN_DIM, 2 * CONV_CH, 2 * d)
    offs = [0]
    for wd_ in widths:
        offs.append(offs[-1] + wd_)
    wi = w_in[0]
    col = lambda k: wi[:, offs[k]:offs[k + 1]]
    w_iw = jnp.pad(col(1), ((0, 0), (0, LANES - IDX_HEADS)))
    w2 = bf(jnp.concatenate([col(0), col(3), col(4), col(5), col(2), col(2), w_iw], axis=1))
    merge_w = (vec(mix_pre_g[0]), bf(col(6)), bf(col(7)), vec(b_gate[0]), bf(w_attn_o[0]),
               conv_dw_w[0], vec(conv_dw_b[0]), vec(conv_ln_g[0]), vec(conv_ln_b[0]),
               bf(w_conv_o[0]), bf(w_out[0]), vec(mix_post_g[0]))

    pos_p = jnp.arange(lp, dtype=I32)
    pos_s = jnp.tile(past + jnp.arange(ts, dtype=I32), db)

    hp = _ffn_half(x_prompt.reshape(lp, d), *ffn1)
    iq, q, kf, kb, vf, vb, ikf, ik2, iw = _mixer_proj(hp, vec(mix_pre_g[0]), w2, _rope_angles(pos_p))
    a = _prompt_attention(q, iq, iw, kb, vb, ik2, topk_p)
    hp, conv_p = _branch_merge(hp, a, jnp.zeros((1, HIST, CONV_CH), F32), *merge_w,
                               nseq=1, seqlen=512, carry=True)
    hp = _ffn_half(hp, *ffn2)
    outs_p = (kf.reshape(1, 1, lp, N_HEADS, HEAD_DIM), vf.reshape(1, 1, lp, N_HEADS, HEAD_DIM),
              ikf.reshape(1, 1, lp, IDX_DIM), conv_p.reshape(1, 1, HIST, CONV_CH))

    n_s = db * ts
    hs = _ffn_half(x_sample.reshape(n_s, d), *ffn1)
    iq, q, kf, kb, vf, vb, ikf, ik2, iw = _mixer_proj(hs, vec(mix_pre_g[0]), w2, _rope_angles(pos_s))
    a = _sample_attention(q, iq, iw, kb, vb, ik2, cache_k[0].reshape(db, past, ATTN_DIM),
                          cache_v[0].reshape(db, past, ATTN_DIM), cache_idx_k[0], ts, topk_s)
    hs, conv_s = _branch_merge(hs, a, state_conv[0], *merge_w, nseq=8, seqlen=ts, carry=False)
    hs = _ffn_half(hs, *ffn2)
    outs_s = (kf.reshape(1, db, ts, N_HEADS, HEAD_DIM), vf.reshape(1, db, ts, N_HEADS, HEAD_DIM),
              ikf.reshape(1, db, ts, IDX_DIM), conv_s.reshape(1, db, HIST, CONV_CH))

    return (hp.reshape(1, lp, d), hs.reshape(db, ts, d)) + outs_p + outs_s
```

```python
import functools
import math

import jax
import jax.numpy as jnp
from jax import lax
from jax.experimental import pallas as pl
from jax.experimental.pallas import tpu as pltpu

F32 = jnp.float32
BF16 = jnp.bfloat16
I32 = jnp.int32

CHUNK = 64
N_HEADS = 8
HEAD_DIM = 64
ATTN_DIM = N_HEADS * HEAD_DIM
IDX_HEADS = 8
IDX_DIM = 64
IDX_W_SCALE = (IDX_HEADS * IDX_DIM) ** -0.5
TOPK_MAX = 256
ROPE_THETA = 500000.0
ROT = HEAD_DIM // 4
CONV_CH = 512
CONV_WIDTH = 31
HIST = CONV_WIDTH - 1
EPS = 1e-6

LANES = 128
VMEM_LIMIT = 56 * 1024 * 1024
INT_MIN = -(2 ** 31)
NEG_BIG = -1e30
Q_SCALE = HEAD_DIM ** -0.5 * math.log2(math.e)

_NT = (((1,), (1,)), ((), ()))


def _rms(x, g):
    return x * lax.rsqrt(jnp.mean(x * x, axis=-1, keepdims=True) + EPS) * g


def _sigmoid(x):
    return 1.0 / (1.0 + jnp.exp(-x))


def _const_spec(shape):
    nd = len(shape)
    return pl.BlockSpec(shape, lambda *_: (0,) * nd, pipeline_mode=pl.Buffered(1))


def _ffn_kernel(x_ref, pre_g_ref, wg_ref, wu_ref, wd_ref, post_g_ref, o_ref):
    x = x_ref[...]
    h = _rms(x, pre_g_ref[...]).astype(BF16)
    g = jnp.dot(h, wg_ref[...], preferred_element_type=F32)
    u = jnp.dot(h, wu_ref[...], preferred_element_type=F32)
    a = (g * _sigmoid(g) * u).astype(BF16)
    f = jnp.dot(a, wd_ref[...], preferred_element_type=F32)
    o_ref[...] = x + 0.5 * _rms(f, post_g_ref[...])


def _ffn_half(x, pre_g, wg, wu, wd, post_g, tm=512):
    t, d = x.shape
    dff = wg.shape[1]
    row = pl.BlockSpec((tm, d), lambda i: (i, 0))
    return pl.pallas_call(
        _ffn_kernel,
        out_shape=jax.ShapeDtypeStruct((t, d), F32),
        grid=(t // tm,),
        in_specs=[row, _const_spec((1, d)), _const_spec((d, dff)), _const_spec((d, dff)),
                  _const_spec((dff, d)), _const_spec((1, d))],
        out_specs=row,
        compiler_params=pltpu.CompilerParams(dimension_semantics=("parallel",),
                                             vmem_limit_bytes=VMEM_LIMIT),
        name="ffn_half",
    )(x, pre_g, wg, wu, wd, post_g)


def _rope_angles(pos):
    half = ROT // 2
    inv_freq = jnp.exp(-math.log(ROPE_THETA) * jnp.arange(half, dtype=F32) * (2.0 / ROT))
    ang = pos.astype(F32)[:, None] * inv_freq[None, :]
    return jnp.pad(jnp.concatenate([jnp.cos(ang), jnp.sin(ang)], axis=1), ((0, 0), (0, LANES - ROT)))


def _rope_tables(cs):
    half = ROT // 2
    lane = lax.broadcasted_iota(I32, cs.shape, 1)
    both = lambda a: a + pltpu.roll(a, HEAD_DIM, 1)
    cos_lo = jnp.where(lane < half, cs, 0.0)
    sin_hi = jnp.where(jnp.logical_and(lane >= half, lane < ROT), cs, 0.0)
    c = both(cos_lo + pltpu.roll(cos_lo, half, 1))
    c = jnp.where(lane % HEAD_DIM < ROT, c, 1.0)
    s_dn = both(sin_hi)
    s_up = -both(pltpu.roll(sin_hi, LANES - half, 1))
    return c, s_up, s_dn


def _proj_kernel(x_ref, g_ref, w_ref, cs_ref,
                 iq_o, q_o, kf_o, kb_o, vf_o, vb_o, ikf_o, ik2_o, iw_o):
    n = _rms(x_ref[...], g_ref[...]).astype(BF16)
    p = jnp.dot(n, w_ref[...], preferred_element_type=F32)
    c, s_up, s_dn = _rope_tables(cs_ref[...])

    def rope(x):
        return x * c + pltpu.roll(x, LANES - ROT // 2, 1) * s_up + pltpu.roll(x, ROT // 2, 1) * s_dn

    a = ATTN_DIM
    for gidx in range(a // LANES):
        sl = slice(gidx * LANES, (gidx + 1) * LANES)
        iq_o[:, sl] = rope(p[:, sl]).astype(BF16)
        q_o[:, sl] = (rope(p[:, a + gidx * LANES:a + (gidx + 1) * LANES]) * Q_SCALE).astype(BF16)
        k = rope(p[:, 2 * a + gidx * LANES:2 * a + (gidx + 1) * LANES])
        kf_o[:, sl] = k
        kb_o[:, sl] = k.astype(BF16)
        v = p[:, 3 * a + gidx * LANES:3 * a + (gidx + 1) * LANES]
        vf_o[:, sl] = v
        vb_o[:, sl] = v.astype(BF16)
    ik = rope(p[:, 4 * a:4 * a + LANES])
    ikf_o[...] = ik[:, :IDX_DIM]
    ik2_o[...] = ik.astype(BF16)
    iw_o[...] = p[:, 4 * a + LANES:4 * a + 2 * LANES] * IDX_W_SCALE


def _mixer_proj(x, g, w2, angles, tm=512):
    t, d = x.shape
    ncol = w2.shape[1]
    row = lambda width: pl.BlockSpec((tm, width), lambda i: (i, 0))
    a = ATTN_DIM
    outs = [
        jax.ShapeDtypeStruct((t, a), BF16),
        jax.ShapeDtypeStruct((t, a), BF16),
        jax.ShapeDtypeStruct((t, a), F32),
        jax.ShapeDtypeStruct((t, a), BF16),
        jax.ShapeDtypeStruct((t, a), F32),
        jax.ShapeDtypeStruct((t, a), BF16),
        jax.ShapeDtypeStruct((t, IDX_DIM), F32),
        jax.ShapeDtypeStruct((t, LANES), BF16),
        jax.ShapeDtypeStruct((t, LANES), F32),
    ]
    return pl.pallas_call(
        _proj_kernel,
        out_shape=outs,
        grid=(t // tm,),
        in_specs=[row(d), _const_spec((1, d)), _const_spec((d, ncol)), row(LANES)],
        out_specs=[row(a), row(a), row(a), row(a), row(a), row(a), row(IDX_DIM), row(LANES), row(LANES)],
        compiler_params=pltpu.CompilerParams(dimension_semantics=("parallel",),
                                             vmem_limit_bytes=VMEM_LIMIT),
        name="mixer_proj",
    )(x, g, w2, angles)


def _sortable(x):
    b = lax.bitcast_convert_type(x, I32)
    k = b ^ ((b >> 31) & 0x7FFFFFFF)
    return jnp.where(x == 0.0, 0, k)


def _head_lhs(x, h):
    grp, odd = divmod(h, 2)
    xg = x[:, grp * LANES:(grp + 1) * LANES].astype(F32)
    lane = lax.broadcasted_iota(I32, xg.shape, 1)
    keep = (lane >= HEAD_DIM) if odd else (lane < HEAD_DIM)
    return jnp.where(keep, xg, 0.0).astype(BF16)


def _stack_heads(x_ref, dst_ref, rows):
    x = x_ref[...]
    for h in range(N_HEADS):
        dst_ref[h * rows:(h + 1) * rows, :] = _head_lhs(x, h)


def _spread_weights(w, wrep_ref):
    for h in range(IDX_HEADS):
        wrep_ref[h] = jnp.broadcast_to(w[:, h:h + 1], wrep_ref.shape[1:])


def _index_dots(iqs_ref, ikc):
    return lax.dot_general(iqs_ref[...], ikc, _NT, preferred_element_type=F32)


def _index_combine(d, wrep_ref):
    rows = wrep_ref.shape[1]
    groups = []
    for l in range(d.shape[1] // LANES):
        acc = None
        for h in range(IDX_HEADS):
            term = jnp.maximum(d[h * rows:(h + 1) * rows, l * LANES:(l + 1) * LANES], 0.0) * wrep_ref[h]
            acc = term if acc is None else acc + term
        groups.append(acc)
    return jnp.concatenate(groups, axis=1)


def _index_scores(iqs_ref, wrep_ref, ikc):
    return _index_combine(_index_dots(iqs_ref, ikc), wrep_ref)


def _lane_fold(x):
    out = x[:, :LANES]
    for l in range(1, x.shape[1] // LANES):
        out = out + x[:, l * LANES:(l + 1) * LANES]
    return out


COUNT_ROWS = 32


def _count_tile(tile_ref, cand):
    rows = tile_ref.shape[0]
    step = min(COUNT_ROWS, rows)
    parts = []
    for r in range(0, rows, step):
        hit = jnp.where(tile_ref[r:r + step, :] >= cand[r:r + step], 1.0, 0.0)
        parts.append(_lane_fold(hit))
    return jnp.concatenate(parts, axis=0)


def _select_threshold(count_ge, n_adm, topk):
    c0 = count_ge(jnp.zeros(n_adm.shape, I32))
    c1 = count_ge(jnp.ones(n_adm.shape, I32))
    nonneg = c0 >= topk
    zero_tie = jnp.logical_and(nonneg, c1 < topk)

    def cond(st):
        i, _, cnt = st
        return jnp.logical_and(i < 32, jnp.max(jnp.where(zero_tie, 0.0, cnt)) > topk)

    def one_bit(i, t, cnt):
        cand = t + jnp.left_shift(jnp.int32(1), 31 - i)
        c = count_ge(cand)
        ok = c >= topk
        return jnp.where(ok, cand, t), jnp.where(ok, c, cnt)

    def step(st):
        i, t, cnt = st
        t, cnt = one_bit(i, t, cnt)
        t, cnt = one_bit(i + 1, t, cnt)
        return i + 2, t, cnt

    t, cnt = one_bit(1, jnp.where(nonneg, 0, INT_MIN), jnp.where(nonneg, c0, n_adm))
    _, t, cnt = lax.while_loop(cond, step, (jnp.int32(2), t, cnt))
    return jnp.maximum(t, INT_MIN + 1), cnt


def _bias_plain(key, t):
    return jnp.where(key >= t, 0.0, NEG_BIG)


def _init_tri(tri_ref):
    @pl.when(pl.program_id(0) == 0)
    def _():
        r = lax.broadcasted_iota(I32, tri_ref.shape, 0)
        c = lax.broadcasted_iota(I32, tri_ref.shape, 1)
        tri_ref[...] = jnp.where(jnp.logical_or(r < c, c >= LANES), 1.0, 0.0).astype(BF16)


def _bias_ties(key, t, need, run, tri_ref):
    rows, sc = key.shape
    groups = sc // LANES
    eq = key == t
    eqb = jnp.where(eq, 1.0, 0.0).astype(BF16)
    stacked = jnp.concatenate([eqb[:, l * LANES:(l + 1) * LANES] for l in range(groups)], axis=0)
    counts = jnp.dot(stacked, tri_ref[...], preferred_element_type=F32)
    before = []
    for l in range(groups):
        blk = counts[l * rows:(l + 1) * rows]
        before.append(run + blk[:, :LANES])
        run = run + blk[:, LANES:]
    sel = (key > t) | (eq & (jnp.concatenate(before, axis=1) < need))
    return jnp.where(sel, 0.0, NEG_BIG), run


def _pair_scores(qs_ref, kc):
    pairs = N_HEADS // 2
    prow = qs_ref.shape[0] // pairs
    return [lax.dot_general(qs_ref[g * prow:(g + 1) * prow, :], kc[:, g * LANES:(g + 1) * LANES], _NT,
                            preferred_element_type=F32) for g in range(pairs)]


def _softmax_pv(scores, bias, vc, m_ref, acc_ref):
    pairs = N_HEADS // 2
    nrep = vc.shape[0] // LANES
    bias2 = jnp.concatenate([bias, bias], axis=0)
    ones = jnp.ones((vc.shape[0], LANES), BF16)
    probs, alphas = [], []
    for g in range(pairs):
        s = scores[g] + bias2
        m_prev = m_ref[g]
        m_next = jnp.maximum(m_prev, jnp.max(s, axis=1, keepdims=True))
        alphas.append(jnp.exp2(m_prev - m_next))
        probs.append(jnp.exp2(s - jnp.concatenate([m_next] * nrep, axis=1)).astype(BF16))
        m_ref[g] = m_next
    for g in range(pairs):
        vext = jnp.concatenate([vc[:, g * LANES:(g + 1) * LANES], ones], axis=1)
        pv = jnp.dot(probs[g], vext, preferred_element_type=F32)
        acc_ref[g] = jnp.concatenate([alphas[g], alphas[g]], axis=1) * acc_ref[g] + pv


def _attend_chunk(qs_ref, bias, kc, vc, m_ref, acc_ref):
    _softmax_pv(_pair_scores(qs_ref, kc), bias, vc, m_ref, acc_ref)


def _init_attend(m_ref, acc_ref):
    m_ref[...] = jnp.full(m_ref.shape, NEG_BIG, F32)
    acc_ref[...] = jnp.zeros(acc_ref.shape, F32)


def _finish_attend(o_ref, acc_ref):
    rows = acc_ref.shape[1] // 2
    lane = lax.broadcasted_iota(I32, (rows, LANES), 1)
    for g in range(N_HEADS // 2):
        acc = acc_ref[g]
        o = acc[:, :LANES] / acc[:, LANES:]
        o_ref[:, g * LANES:(g + 1) * LANES] = jnp.where(lane < HEAD_DIM, o[:rows], o[rows:]).astype(o_ref.dtype)


def _prompt_attn_kernel(q_ref, iq_ref, iw_ref, kb_ref, vb_ref, ik2_ref, o_ref,
                        key_ref, iqs_ref, qs_ref, wrep_ref, m_ref, acc_ref, tri_ref, *, tq, sc, topk):
    j = pl.program_id(0)
    n_chunks = ((j + 1) * tq + sc - 1) // sc

    _init_tri(tri_ref)
    _stack_heads(iq_ref, iqs_ref, tq)
    _stack_heads(q_ref, qs_ref, tq)
    _spread_weights(iw_ref[...], wrep_ref)
    row = lax.broadcasted_iota(I32, (tq, 1), 0)
    limit = j * tq + (row // CHUNK + 1) * CHUNK

    def score_pair(i, carry):
        chunks = (2 * i, jnp.minimum(2 * i + 1, n_chunks - 1))
        offs = [pl.multiple_of(c * sc, sc) for c in chunks]
        dots = [_index_dots(iqs_ref, ik2_ref[pl.ds(off, sc), :]) for off in offs]
        for c, off, d in zip(chunks, offs, dots):
            sidx = off + lax.broadcasted_iota(I32, (tq, sc), 1)
            key_ref[c] = jnp.where(sidx < limit, _sortable(_index_combine(d, wrep_ref)), INT_MIN)
        return carry
    lax.fori_loop(0, (n_chunks + 1) // 2, score_pair, 0)

    def count_ge(cand):
        def body(c, acc):
            return acc + _count_tile(key_ref.at[c], cand)
        part = lax.fori_loop(0, n_chunks, body, jnp.zeros((tq, LANES), F32))
        return jnp.sum(part, axis=1, keepdims=True)

    t, n_ge = _select_threshold(count_ge, limit.astype(F32), topk)
    has_ties = jnp.max(n_ge) > topk

    @pl.when(jnp.logical_not(has_ties))
    def _():
        def body(c, carry):
            key_ref[c] = lax.bitcast_convert_type(_bias_plain(key_ref[c], t), I32)
            return carry
        lax.fori_loop(0, n_chunks, body, 0)

    @pl.when(has_ties)
    def _():
        need = topk - count_ge(t + 1)
        def body(c, run):
            bias, run = _bias_ties(key_ref[c], t, need, run, tri_ref)
            key_ref[c] = lax.bitcast_convert_type(bias, I32)
            return run
        lax.fori_loop(0, n_chunks, body, jnp.zeros((tq, LANES), F32))

    _init_attend(m_ref, acc_ref)

    def attend(c, carry):
        off = pl.multiple_of(c * sc, sc)
        bias = lax.bitcast_convert_type(key_ref[c], F32)
        _attend_chunk(qs_ref, bias, kb_ref[pl.ds(off, sc), :], vb_ref[pl.ds(off, sc), :],
                      m_ref, acc_ref)
        return carry
    lax.fori_loop(0, n_chunks, attend, 0)
    _finish_attend(o_ref, acc_ref)


def _attn_scratch(rows):
    pairs = N_HEADS // 2
    return [
        pltpu.VMEM((N_HEADS * rows, LANES), BF16),
        pltpu.VMEM((N_HEADS * rows, LANES), BF16),
        pltpu.VMEM((IDX_HEADS, rows, LANES), F32),
        pltpu.VMEM((pairs, 2 * rows, LANES), F32),
        pltpu.VMEM((pairs, 2 * rows, 2 * LANES), F32),
        pltpu.VMEM((LANES, 2 * LANES), BF16),
    ]


def _prompt_attention(q, iq, iw, kb, vb, ik2, topk, tq=128, sc=512):
    s = kb.shape[0]
    row = lambda width: pl.BlockSpec((tq, width), lambda j: (j, 0))
    whole = pl.BlockSpec(memory_space=pltpu.VMEM)
    return pl.pallas_call(
        functools.partial(_prompt_attn_kernel, tq=tq, sc=sc, topk=topk),
        out_shape=jax.ShapeDtypeStruct((s, ATTN_DIM), BF16),
        grid=(s // tq,),
        in_specs=[row(ATTN_DIM), row(ATTN_DIM), row(LANES), whole, whole, whole],
        out_specs=row(ATTN_DIM),
        scratch_shapes=[pltpu.VMEM((s // sc, tq, sc), I32)] + _attn_scratch(tq),
        compiler_params=pltpu.CompilerParams(dimension_semantics=("arbitrary",),
                                             vmem_limit_bytes=VMEM_LIMIT),
        name="prompt_attention",
    )(q, iq, iw, kb, vb, ik2)


def _sample_attn_kernel(q_ref, iq_ref, iw_ref, kn_ref, vn_ref, ikn_ref, ck_ref, cv_ref, cik_ref, o_ref,
                        key_ref, keyn_ref, iqs_ref, qs_ref, wrep_ref, m_ref, acc_ref, tri_ref, *, sc, topk):
    ts = q_ref.shape[0]
    past = ck_ref.shape[1]
    n_chunks = past // sc

    _init_tri(tri_ref)
    _stack_heads(iq_ref, iqs_ref, ts)
    _stack_heads(q_ref, qs_ref, ts)
    _spread_weights(iw_ref[...], wrep_ref)

    pad = lambda x: jnp.concatenate([x, jnp.zeros((LANES - ts, x.shape[1]), x.dtype)], axis=0)
    new_ok = lax.broadcasted_iota(I32, (ts, LANES), 1) < ts

    def cache_ik(c):
        ik = cik_ref[0, c * sc:(c + 1) * sc, :].astype(BF16)
        return jnp.concatenate([ik, ik], axis=1)

    for c in range(n_chunks):
        key_ref[c] = _sortable(_index_scores(iqs_ref, wrep_ref, cache_ik(c)))
    keyn_ref[...] = jnp.where(new_ok, _sortable(_index_scores(iqs_ref, wrep_ref, pad(ikn_ref[...]))), INT_MIN)

    def count_ge(cand):
        part = jnp.where(keyn_ref[...] >= cand, 1.0, 0.0)
        for c in range(n_chunks):
            part = part + _count_tile(key_ref.at[c], cand)
        return jnp.sum(part, axis=1, keepdims=True)

    t, n_ge = _select_threshold(count_ge, jnp.full((ts, 1), past + ts, F32), topk)
    has_ties = jnp.max(n_ge) > topk

    @pl.when(jnp.logical_not(has_ties))
    def _():
        for c in range(n_chunks):
            key_ref[c] = lax.bitcast_convert_type(_bias_plain(key_ref[c], t), I32)
        keyn_ref[...] = lax.bitcast_convert_type(_bias_plain(keyn_ref[...], t), I32)

    @pl.when(has_ties)
    def _():
        need = topk - count_ge(t + 1)
        run = jnp.zeros((ts, LANES), F32)
        for c in range(n_chunks):
            bias, run = _bias_ties(key_ref[c], t, need, run, tri_ref)
            key_ref[c] = lax.bitcast_convert_type(bias, I32)
        bias, run = _bias_ties(keyn_ref[...], t, need, run, tri_ref)
        keyn_ref[...] = lax.bitcast_convert_type(bias, I32)

    _init_attend(m_ref, acc_ref)
    for c in range(n_chunks):
        _attend_chunk(qs_ref,lax.bitcast_convert_type(key_ref[c], F32),
                      ck_ref[0, c * sc:(c + 1) * sc, :].astype(BF16),
                      cv_ref[0, c * sc:(c + 1) * sc, :].astype(BF16), m_ref, acc_ref)
    _attend_chunk(qs_ref,lax.bitcast_convert_type(keyn_ref[...], F32),
                  pad(kn_ref[...]), pad(vn_ref[...]), m_ref, acc_ref)
    _finish_attend(o_ref, acc_ref)


def _sample_attention(q, iq, iw, kb, vb, ik2, cache_k, cache_v, cache_ik, ts, topk, sc=512):
    t = q.shape[0]
    nb, past = cache_k.shape[:2]
    row = lambda width: pl.BlockSpec((ts, width), lambda b: (b, 0))
    cache = lambda width: pl.BlockSpec((1, past, width), lambda b: (b, 0, 0))
    return pl.pallas_call(
        functools.partial(_sample_attn_kernel, sc=sc, topk=topk),
        out_shape=jax.ShapeDtypeStruct((t, ATTN_DIM), BF16),
        grid=(nb,),
        in_specs=[row(ATTN_DIM), row(ATTN_DIM), row(LANES), row(ATTN_DIM), row(ATTN_DIM), row(LANES),
                  cache(ATTN_DIM), cache(ATTN_DIM), cache(IDX_DIM)],
        out_specs=row(ATTN_DIM),
        scratch_shapes=([pltpu.VMEM((past // sc, ts, sc), I32), pltpu.VMEM((ts, LANES), I32)]
                        + _attn_scratch(ts)),
        compiler_params=pltpu.CompilerParams(dimension_semantics=("arbitrary",),
                                             vmem_limit_bytes=VMEM_LIMIT),
        name="sample_attention",
    )(q, iq, iw, kb, vb, ik2, cache_k, cache_v, cache_ik)


HIST_ROW0 = 2


def _merge_kernel(x_ref, a_ref, hist_ref, g_ref, wglu_ref, wgate_ref, bgate_ref, wa_ref,
                  dww_ref, dwb_ref, lng_ref, lnb_ref, wc_ref, wo_ref, postg_ref,
                  y_ref, conv_o_ref, uext_ref, *, nseq, seqlen, carry):
    i = pl.program_id(0)
    x = x_ref[...]
    n = _rms(x, g_ref[...]).astype(BF16)
    glu = jnp.dot(n, wglu_ref[...], preferred_element_type=F32)
    u = glu[:, :CONV_CH] * _sigmoid(glu[:, CONV_CH:])

    new0 = HIST_ROW0 + HIST
    if carry:
        @pl.when(i == 0)
        def _():
            uext_ref[:, HIST_ROW0:new0, :] = hist_ref[...]

        @pl.when(i > 0)
        def _():
            uext_ref[:, HIST_ROW0:new0, :] = uext_ref[:, seqlen + HIST_ROW0:seqlen + new0, :]
    else:
        uext_ref[:, HIST_ROW0:new0, :] = hist_ref[...]
    uext_ref[:, new0:new0 + seqlen, :] = u.reshape(nseq, seqlen, CONV_CH)
    conv_o_ref[...] = uext_ref[:, seqlen + HIST_ROW0:seqlen + new0, :]

    dww = dww_ref[...]
    cdw = None
    for tap in range(CONV_WIDTH):
        term = uext_ref[:, HIST_ROW0 + tap:HIST_ROW0 + tap + seqlen, :] * dww[tap:tap + 1, :]
        cdw = term if cdw is None else cdw + term
    cdw = cdw.reshape(nseq * seqlen, CONV_CH) + dwb_ref[...]

    mu = jnp.mean(cdw, axis=-1, keepdims=True)
    xc = cdw - mu
    var = jnp.mean(xc * xc, axis=-1, keepdims=True)
    ln = xc * lax.rsqrt(var + EPS) * lng_ref[...] + lnb_ref[...]
    cact = (ln * _sigmoid(ln)).astype(BF16)
    c_proj = jnp.dot(cact, wc_ref[...], preferred_element_type=F32)
    a_proj = jnp.dot(a_ref[...], wa_ref[...], preferred_element_type=F32)

    d = x.shape[1]
    gate = _sigmoid(jnp.dot(n, wgate_ref[...], preferred_element_type=F32) + bgate_ref[...])
    merged = (gate[:, :d] * a_proj + gate[:, d:] * c_proj).astype(BF16)
    y = jnp.dot(merged, wo_ref[...], preferred_element_type=F32)
    y_ref[...] = x + _rms(y, postg_ref[...])


def _branch_merge(x, a, hist, g, wglu, wgate, bgate, wa, dww, dwb, lng, lnb, wc, wo, postg,
                  *, nseq, seqlen, carry):
    t, d = x.shape
    tm = nseq * seqlen
    row = lambda width: pl.BlockSpec((tm, width), lambda i: (i, 0))
    if carry:
        seq_spec = pl.BlockSpec((nseq, HIST, CONV_CH), lambda i: (0, 0, 0))
        n_state = nseq
    else:
        seq_spec = pl.BlockSpec((nseq, HIST, CONV_CH), lambda i: (i, 0, 0))
        n_state = t // seqlen
    consts = [g, wglu, wgate, bgate, wa, dww, dwb, lng, lnb, wc, wo, postg]
    return pl.pallas_call(
        functools.partial(_merge_kernel, nseq=nseq, seqlen=seqlen, carry=carry),
        out_shape=[jax.ShapeDtypeStruct((t, d), F32),
                   jax.ShapeDtypeStruct((n_state, HIST, CONV_CH), F32)],
        grid=(t // tm,),
        in_specs=[row(d), row(ATTN_DIM), seq_spec] + [_const_spec(c.shape) for c in consts],
        out_specs=[row(d), seq_spec],
        scratch_shapes=[pltpu.VMEM((nseq, HIST_ROW0 + HIST + seqlen, CONV_CH), F32)],
        compiler_params=pltpu.CompilerParams(dimension_semantics=("arbitrary",),
                                             vmem_limit_bytes=VMEM_LIMIT),
        name="branch_merge",
    )(x, a, hist, *consts)


def kernel(x_prompt, x_sample, cache_k, cache_v, cache_idx_k, state_conv,
           ffn1_pre_g, ffn1_w_gate, ffn1_w_up, ffn1_w_down, ffn1_post_g,
           mix_pre_g, w_in, b_gate, w_attn_o, conv_dw_w, conv_dw_b, conv_ln_g, conv_ln_b,
           w_conv_o, w_out, mix_post_g,
           ffn2_pre_g, ffn2_w_gate, ffn2_w_up, ffn2_w_down, ffn2_post_g):
    bp, lp, d = x_prompt.shape
    db, ts, _ = x_sample.shape
    depth, _, past = cache_k.shape[:3]
    assert bp == 1 and depth == 1
    topk_p = min(TOPK_MAX, lp // 4)
    topk_s = min(TOPK_MAX, (past + ts) // 4)

    vec = lambda a: a.reshape(1, -1)
    bf = lambda a: a.astype(BF16)
    ffn1 = (vec(ffn1_pre_g[0]), bf(ffn1_w_gate[0]), bf(ffn1_w_up[0]), bf(ffn1_w_down[0]), vec(ffn1_post_g[0]))
    ffn2 = (vec(ffn2_pre_g[0]), bf(ffn2_w_gate[0]), bf(ffn2_w_up[0]), bf(ffn2_w_down[0]), vec(ffn2_post_g[0]))

    widths = (IDX_HEADS * IDX_DIM, IDX_HEADS, IDX_DIM, ATTN_DIM, ATTN_DIM, ATTN_DIM, 2 * CONV_CH, 2 * d)
    offs = [0]
    for wd_ in widths:
        offs.append(offs[-1] + wd_)
    wi = w_in[0]
    col = lambda k: wi[:, offs[k]:offs[k + 1]]
    w_iw = jnp.pad(col(1), ((0, 0), (0, LANES - IDX_HEADS)))
    w2 = bf(jnp.concatenate([col(0), col(3), col(4), col(5), col(2), col(2), w_iw], axis=1))
    merge_w = (vec(mix_pre_g[0]), bf(col(6)), bf(col(7)), vec(b_gate[0]), bf(w_attn_o[0]),
               conv_dw_w[0], vec(conv_dw_b[0]), vec(conv_ln_g[0]), vec(conv_ln_b[0]),
               bf(w_conv_o[0]), bf(w_out[0]), vec(mix_post_g[0]))

    pos_p = jnp.arange(lp, dtype=I32)
    pos_s = jnp.tile(past + jnp.arange(ts, dtype=I32), db)

    hp = _ffn_half(x_prompt.reshape(lp, d), *ffn1)
    iq, q, kf, kb, vf, vb, ikf, ik2, iw = _mixer_proj(hp, vec(mix_pre_g[0]), w2, _rope_angles(pos_p))
    a = _prompt_attention(q, iq, iw, kb, vb, ik2, topk_p)
    hp, conv_p = _branch_merge(hp, a, jnp.zeros((1, HIST, CONV_CH), F32), *merge_w,
                               nseq=1, seqlen=512, carry=True)
    hp = _ffn_half(hp, *ffn2)
    outs_p = (kf.reshape(1, 1, lp, N_HEADS, HEAD_DIM), vf.reshape(1, 1, lp, N_HEADS, HEAD_DIM),
              ikf.reshape(1, 1, lp, IDX_DIM), conv_p.reshape(1, 1, HIST, CONV_CH))

    n_s = db * ts
    hs = _ffn_half(x_sample.reshape(n_s, d), *ffn1)
    iq, q, kf, kb, vf, vb, ikf, ik2, iw = _mixer_proj(hs, vec(mix_pre_g[0]), w2, _rope_angles(pos_s))
    a = _sample_attention(q, iq, iw, kb, vb, ik2, cache_k[0].reshape(db, past, ATTN_DIM),
                          cache_v[0].reshape(db, past, ATTN_DIM), cache_idx_k[0], ts, topk_s)
    hs, conv_s = _branch_merge(hs, a, state_conv[0], *merge_w, nseq=8, seqlen=ts, carry=False)
    hs = _ffn_half(hs, *ffn2)
    outs_s = (kf.reshape(1, db, ts, N_HEADS, HEAD_DIM), vf.reshape(1, db, ts, N_HEADS, HEAD_DIM),
              ikf.reshape(1, db, ts, IDX_DIM), conv_s.reshape(1, db, HIST, CONV_CH))

    return (hp.reshape(1, lp, d), hs.reshape(db, ts, d)) + outs_p + outs_s
```

```python
import functools
import math

import jax
import jax.numpy as jnp
from jax import lax
from jax.experimental import pallas as pl
from jax.experimental.pallas import tpu as pltpu

F32 = jnp.float32
BF16 = jnp.bfloat16
I32 = jnp.int32

CHUNK = 64
N_HEADS = 8
HEAD_DIM = 64
ATTN_DIM = N_HEADS * HEAD_DIM
IDX_HEADS = 8
IDX_DIM = 64
IDX_W_SCALE = (IDX_HEADS * IDX_DIM) ** -0.5
TOPK_MAX = 256
ROPE_THETA = 500000.0
ROT = HEAD_DIM // 4
CONV_CH = 512
CONV_WIDTH = 31
HIST = CONV_WIDTH - 1
EPS = 1e-6

LANES = 128
VMEM_LIMIT = 56 * 1024 * 1024
INT_MIN = -(2 ** 31)
NEG_BIG = -1e30
Q_SCALE = HEAD_DIM ** -0.5 * math.log2(math.e)

_NT = (((1,), (1,)), ((), ()))


def _rms(x, g):
    return x * lax.rsqrt(jnp.mean(x * x, axis=-1, keepdims=True) + EPS) * g


def _sigmoid(x):
    return 1.0 / (1.0 + jnp.exp(-x))


def _const_spec(shape):
    nd = len(shape)
    return pl.BlockSpec(shape, lambda *_: (0,) * nd, pipeline_mode=pl.Buffered(1))


def _ffn_kernel(x_ref, pre_g_ref, wg_ref, wu_ref, wd_ref, post_g_ref, o_ref):
    x = x_ref[...]
    h = _rms(x, pre_g_ref[...]).astype(BF16)
    g = jnp.dot(h, wg_ref[...], preferred_element_type=F32)
    u = jnp.dot(h, wu_ref[...], preferred_element_type=F32)
    a = (g * _sigmoid(g) * u).astype(BF16)
    f = jnp.dot(a, wd_ref[...], preferred_element_type=F32)
    o_ref[...] = x + 0.5 * _rms(f, post_g_ref[...])


def _ffn_half(x, pre_g, wg, wu, wd, post_g, tm=512):
    t, d = x.shape
    dff = wg.shape[1]
    row = pl.BlockSpec((tm, d), lambda i: (i, 0))
    return pl.pallas_call(
        _ffn_kernel,
        out_shape=jax.ShapeDtypeStruct((t, d), F32),
        grid=(t // tm,),
        in_specs=[row, _const_spec((1, d)), _const_spec((d, dff)), _const_spec((d, dff)),
                  _const_spec((dff, d)), _const_spec((1, d))],
        out_specs=row,
        compiler_params=pltpu.CompilerParams(dimension_semantics=("parallel",),
                                             vmem_limit_bytes=VMEM_LIMIT),
        name="ffn_half",
    )(x, pre_g, wg, wu, wd, post_g)


def _rope_angles(pos):
    half = ROT // 2
    inv_freq = jnp.exp(-math.log(ROPE_THETA) * jnp.arange(half, dtype=F32) * (2.0 / ROT))
    ang = pos.astype(F32)[:, None] * inv_freq[None, :]
    return jnp.pad(jnp.concatenate([jnp.cos(ang), jnp.sin(ang)], axis=1), ((0, 0), (0, LANES - ROT)))


def _rope_tables(cs):
    half = ROT // 2
    lane = lax.broadcasted_iota(I32, cs.shape, 1)
    both = lambda a: a + pltpu.roll(a, HEAD_DIM, 1)
    cos_lo = jnp.where(lane < half, cs, 0.0)
    sin_hi = jnp.where(jnp.logical_and(lane >= half, lane < ROT), cs, 0.0)
    c = both(cos_lo + pltpu.roll(cos_lo, half, 1))
    c = jnp.where(lane % HEAD_DIM < ROT, c, 1.0)
    s_dn = both(sin_hi)
    s_up = -both(pltpu.roll(sin_hi, LANES - half, 1))
    return c, s_up, s_dn


def _proj_kernel(x_ref, g_ref, w_ref, cs_ref,
                 iq_o, q_o, kf_o, kb_o, vf_o, vb_o, ikf_o, ik2_o, iw_o):
    n = _rms(x_ref[...], g_ref[...]).astype(BF16)
    p = jnp.dot(n, w_ref[...], preferred_element_type=F32)
    c, s_up, s_dn = _rope_tables(cs_ref[...])

    def rope(x):
        return x * c + pltpu.roll(x, LANES - ROT // 2, 1) * s_up + pltpu.roll(x, ROT // 2, 1) * s_dn

    a = ATTN_DIM
    for gidx in range(a // LANES):
        sl = slice(gidx * LANES, (gidx + 1) * LANES)
        iq_o[:, sl] = rope(p[:, sl]).astype(BF16)
        q_o[:, sl] = (rope(p[:, a + gidx * LANES:a + (gidx + 1) * LANES]) * Q_SCALE).astype(BF16)
        k = rope(p[:, 2 * a + gidx * LANES:2 * a + (gidx + 1) * LANES])
        kf_o[:, sl] = k
        kb_o[:, sl] = k.astype(BF16)
        v = p[:, 3 * a + gidx * LANES:3 * a + (gidx + 1) * LANES]
        vf_o[:, sl] = v
        vb_o[:, sl] = v.astype(BF16)
    ik = rope(p[:, 4 * a:4 * a + LANES])
    ikf_o[...] = ik[:, :IDX_DIM]
    ik2_o[...] = ik.astype(BF16)
    iw_o[...] = p[:, 4 * a + LANES:4 * a + 2 * LANES] * IDX_W_SCALE


def _mixer_proj(x, g, w2, angles, tm=512):
    t, d = x.shape
    ncol = w2.shape[1]
    row = lambda width: pl.BlockSpec((tm, width), lambda i: (i, 0))
    a = ATTN_DIM
    outs = [
        jax.ShapeDtypeStruct((t, a), BF16),
        jax.ShapeDtypeStruct((t, a), BF16),
        jax.ShapeDtypeStruct((t, a), F32),
        jax.ShapeDtypeStruct((t, a), BF16),
        jax.ShapeDtypeStruct((t, a), F32),
        jax.ShapeDtypeStruct((t, a), BF16),
        jax.ShapeDtypeStruct((t, IDX_DIM), F32),
        jax.ShapeDtypeStruct((t, LANES), BF16),
        jax.ShapeDtypeStruct((t, LANES), F32),
    ]
    return pl.pallas_call(
        _proj_kernel,
        out_shape=outs,
        grid=(t // tm,),
        in_specs=[row(d), _const_spec((1, d)), _const_spec((d, ncol)), row(LANES)],
        out_specs=[row(a), row(a), row(a), row(a), row(a), row(a), row(IDX_DIM), row(LANES), row(LANES)],
        compiler_params=pltpu.CompilerParams(dimension_semantics=("parallel",),
                                             vmem_limit_bytes=VMEM_LIMIT),
        name="mixer_proj",
    )(x, g, w2, angles)


def _sortable(x):
    b = lax.bitcast_convert_type(x, I32)
    k = b ^ ((b >> 31) & 0x7FFFFFFF)
    return jnp.where(x == 0.0, 0, k)


def _head_lhs(x, h):
    grp, odd = divmod(h, 2)
    xg = x[:, grp * LANES:(grp + 1) * LANES].astype(F32)
    lane = lax.broadcasted_iota(I32, xg.shape, 1)
    keep = (lane >= HEAD_DIM) if odd else (lane < HEAD_DIM)
    return jnp.where(keep, xg, 0.0).astype(BF16)


def _stack_heads(x_ref, dst_ref, rows):
    x = x_ref[...]
    for h in range(N_HEADS):
        dst_ref[h * rows:(h + 1) * rows, :] = _head_lhs(x, h)


def _spread_weights(w, wrep_ref):
    for h in range(IDX_HEADS):
        wrep_ref[h] = jnp.broadcast_to(w[:, h:h + 1], wrep_ref.shape[1:])


def _index_dots(iqs_ref, ikc):
    return lax.dot_general(iqs_ref[...], ikc, _NT, preferred_element_type=F32)


def _index_combine(d, wrep_ref):
    rows = wrep_ref.shape[1]
    groups = []
    for l in range(d.shape[1] // LANES):
        acc = None
        for h in range(IDX_HEADS):
            term = jnp.maximum(d[h * rows:(h + 1) * rows, l * LANES:(l + 1) * LANES], 0.0) * wrep_ref[h]
            acc = term if acc is None else acc + term
        groups.append(acc)
    return jnp.concatenate(groups, axis=1)


def _index_scores(iqs_ref, wrep_ref, ikc):
    return _index_combine(_index_dots(iqs_ref, ikc), wrep_ref)


def _lane_fold(x):
    out = x[:, :LANES]
    for l in range(1, x.shape[1] // LANES):
        out = out + x[:, l * LANES:(l + 1) * LANES]
    return out


COUNT_ROWS = 32


def _count_tile(tile_ref, cand):
    rows = tile_ref.shape[0]
    step = min(COUNT_ROWS, rows)
    parts = []
    for r in range(0, rows, step):
        hit = jnp.where(tile_ref[r:r + step, :] >= cand[r:r + step], 1.0, 0.0)
        parts.append(_lane_fold(hit))
    return jnp.concatenate(parts, axis=0)


def _select_threshold(count_ge, n_adm, topk):
    c0 = count_ge(jnp.zeros(n_adm.shape, I32))
    c1 = count_ge(jnp.ones(n_adm.shape, I32))
    nonneg = c0 >= topk
    zero_tie = jnp.logical_and(nonneg, c1 < topk)

    def cond(st):
        i, _, cnt = st
        return jnp.logical_and(i < 32, jnp.max(jnp.where(zero_tie, 0.0, cnt)) > topk)

    def one_bit(i, t, cnt):
        cand = t + jnp.left_shift(jnp.int32(1), 31 - i)
        c = count_ge(cand)
        ok = c >= topk
        return jnp.where(ok, cand, t), jnp.where(ok, c, cnt)

    def step(st):
        i, t, cnt = st
        t, cnt = one_bit(i, t, cnt)
        t, cnt = one_bit(i + 1, t, cnt)
        return i + 2, t, cnt

    t, cnt = one_bit(1, jnp.where(nonneg, 0, INT_MIN), jnp.where(nonneg, c0, n_adm))
    _, t, cnt = lax.while_loop(cond, step, (jnp.int32(2), t, cnt))
    return jnp.maximum(t, INT_MIN + 1), cnt


def _bias_plain(key, t):
    return jnp.where(key >= t, 0.0, NEG_BIG)


def _init_tri(tri_ref):
    @pl.when(pl.program_id(0) == 0)
    def _():
        r = lax.broadcasted_iota(I32, tri_ref.shape, 0)
        c = lax.broadcasted_iota(I32, tri_ref.shape, 1)
        tri_ref[...] = jnp.where(jnp.logical_or(r < c, c >= LANES), 1.0, 0.0).astype(BF16)


def _bias_ties(key, t, need, run, tri_ref):
    rows, sc = key.shape
    groups = sc // LANES
    eq = key == t
    eqb = jnp.where(eq, 1.0, 0.0).astype(BF16)
    stacked = jnp.concatenate([eqb[:, l * LANES:(l + 1) * LANES] for l in range(groups)], axis=0)
    counts = jnp.dot(stacked, tri_ref[...], preferred_element_type=F32)
    before = []
    for l in range(groups):
        blk = counts[l * rows:(l + 1) * rows]
        before.append(run + blk[:, :LANES])
        run = run + blk[:, LANES:]
    sel = (key > t) | (eq & (jnp.concatenate(before, axis=1) < need))
    return jnp.where(sel, 0.0, NEG_BIG), run


def _pair_scores(qs_ref, kc):
    pairs = N_HEADS // 2
    prow = qs_ref.shape[0] // pairs
    return [lax.dot_general(qs_ref[g * prow:(g + 1) * prow, :], kc[:, g * LANES:(g + 1) * LANES], _NT,
                            preferred_element_type=F32) for g in range(pairs)]


def _softmax_pv(scores, bias, vc, m_ref, acc_ref):
    pairs = N_HEADS // 2
    nrep = vc.shape[0] // LANES
    bias2 = jnp.concatenate([bias, bias], axis=0)
    ones = jnp.ones((vc.shape[0], LANES), BF16)
    probs, alphas = [], []
    for g in range(pairs):
        s = scores[g] + bias2
        m_prev = m_ref[g]
        m_next = jnp.maximum(m_prev, jnp.max(s, axis=1, keepdims=True))
        alphas.append(jnp.exp2(m_prev - m_next))
        probs.append(jnp.exp2(s - jnp.concatenate([m_next] * nrep, axis=1)).astype(BF16))
        m_ref[g] = m_next
    for g in range(pairs):
        vext = jnp.concatenate([vc[:, g * LANES:(g + 1) * LANES], ones], axis=1)
        pv = jnp.dot(probs[g], vext, preferred_element_type=F32)
        acc_ref[g] = jnp.concatenate([alphas[g], alphas[g]], axis=1) * acc_ref[g] + pv


def _attend_chunk(qs_ref, bias, kc, vc, m_ref, acc_ref):
    _softmax_pv(_pair_scores(qs_ref, kc), bias, vc, m_ref, acc_ref)


def _init_attend(m_ref, acc_ref):
    m_ref[...] = jnp.full(m_ref.shape, NEG_BIG, F32)
    acc_ref[...] = jnp.zeros(acc_ref.shape, F32)


def _finish_attend(o_ref, acc_ref):
    rows = acc_ref.shape[1] // 2
    lane = lax.broadcasted_iota(I32, (rows, LANES), 1)
    for g in range(N_HEADS // 2):
        acc = acc_ref[g]
        o = acc[:, :LANES] / acc[:, LANES:]
        o_ref[:, g * LANES:(g + 1) * LANES] = jnp.where(lane < HEAD_DIM, o[:rows], o[rows:]).astype(o_ref.dtype)


def _prompt_attn_kernel(q_ref, iq_ref, iw_ref, kb_ref, vb_ref, ik2_ref, o_ref,
                        key_ref, iqs_ref, qs_ref, wrep_ref, m_ref, acc_ref, tri_ref, *, tq, sc, topk):
    j = pl.program_id(0)
    n_chunks = ((j + 1) * tq + sc - 1) // sc

    _init_tri(tri_ref)
    _stack_heads(iq_ref, iqs_ref, tq)
    _stack_heads(q_ref, qs_ref, tq)
    _spread_weights(iw_ref[...], wrep_ref)
    row = lax.broadcasted_iota(I32, (tq, 1), 0)
    limit = j * tq + (row // CHUNK + 1) * CHUNK

    def pairs_then_tail(fn):
        def trip(i, carry):
            fn((2 * i, 2 * i + 1))
            return carry
        lax.fori_loop(0, n_chunks // 2, trip, 0)

        @pl.when(n_chunks % 2 == 1)
        def _():
            fn((n_chunks - 1,))

    def score_chunks(chunks):
        offs = [pl.multiple_of(c * sc, sc) for c in chunks]
        dots = [_index_dots(iqs_ref, ik2_ref[pl.ds(off, sc), :]) for off in offs]
        for c, off, d in zip(chunks, offs, dots):
            sidx = off + lax.broadcasted_iota(I32, (tq, sc), 1)
            key_ref[c] = jnp.where(sidx < limit, _sortable(_index_combine(d, wrep_ref)), INT_MIN)
    pairs_then_tail(score_chunks)

    def count_ge(cand):
        def body(c, acc):
            return acc + _count_tile(key_ref.at[c], cand)
        part = lax.fori_loop(0, n_chunks, body, jnp.zeros((tq, LANES), F32))
        return jnp.sum(part, axis=1, keepdims=True)

    t, n_ge = _select_threshold(count_ge, limit.astype(F32), topk)
    has_ties = jnp.max(n_ge) > topk

    @pl.when(jnp.logical_not(has_ties))
    def _():
        def body(c, carry):
            key_ref[c] = lax.bitcast_convert_type(_bias_plain(key_ref[c], t), I32)
            return carry
        lax.fori_loop(0, n_chunks, body, 0)

    @pl.when(has_ties)
    def _():
        need = topk - count_ge(t + 1)
        def body(c, run):
            bias, run = _bias_ties(key_ref[c], t, need, run, tri_ref)
            key_ref[c] = lax.bitcast_convert_type(bias, I32)
            return run
        lax.fori_loop(0, n_chunks, body, jnp.zeros((tq, LANES), F32))

    _init_attend(m_ref, acc_ref)

    def attend_chunks(chunks):
        offs = [pl.multiple_of(c * sc, sc) for c in chunks]
        scores = [_pair_scores(qs_ref, kb_ref[pl.ds(off, sc), :]) for off in offs]
        for c, off, s in zip(chunks, offs, scores):
            _softmax_pv(s, lax.bitcast_convert_type(key_ref[c], F32), vb_ref[pl.ds(off, sc), :], m_ref, acc_ref)
    pairs_then_tail(attend_chunks)
    _finish_attend(o_ref, acc_ref)


def _attn_scratch(rows):
    pairs = N_HEADS // 2
    return [
        pltpu.VMEM((N_HEADS * rows, LANES), BF16),
        pltpu.VMEM((N_HEADS * rows, LANES), BF16),
        pltpu.VMEM((IDX_HEADS, rows, LANES), F32),
        pltpu.VMEM((pairs, 2 * rows, LANES), F32),
        pltpu.VMEM((pairs, 2 * rows, 2 * LANES), F32),
        pltpu.VMEM((LANES, 2 * LANES), BF16),
    ]


def _prompt_attention(q, iq, iw, kb, vb, ik2, topk, tq=128, sc=512):
    s = kb.shape[0]
    row = lambda width: pl.BlockSpec((tq, width), lambda j: (j, 0))
    whole = pl.BlockSpec(memory_space=pltpu.VMEM)
    return pl.pallas_call(
        functools.partial(_prompt_attn_kernel, tq=tq, sc=sc, topk=topk),
        out_shape=jax.ShapeDtypeStruct((s, ATTN_DIM), BF16),
        grid=(s // tq,),
        in_specs=[row(ATTN_DIM), row(ATTN_DIM), row(LANES), whole, whole, whole],
        out_specs=row(ATTN_DIM),
        scratch_shapes=[pltpu.VMEM((s // sc, tq, sc), I32)] + _attn_scratch(tq),
        compiler_params=pltpu.CompilerParams(dimension_semantics=("arbitrary",),
                                             vmem_limit_bytes=VMEM_LIMIT),
        name="prompt_attention",
    )(q, iq, iw, kb, vb, ik2)


def _sample_attn_kernel(q_ref, iq_ref, iw_ref, kn_ref, vn_ref, ikn_ref, ck_ref, cv_ref, cik_ref, o_ref,
                        key_ref, keyn_ref, iqs_ref, qs_ref, wrep_ref, m_ref, acc_ref, tri_ref, *, sc, topk):
    ts = q_ref.shape[0]
    past = ck_ref.shape[1]
    n_chunks = past // sc

    _init_tri(tri_ref)
    _stack_heads(iq_ref, iqs_ref, ts)
    _stack_heads(q_ref, qs_ref, ts)
    _spread_weights(iw_ref[...], wrep_ref)

    pad = lambda x: jnp.concatenate([x, jnp.zeros((LANES - ts, x.shape[1]), x.dtype)], axis=0)
    new_ok = lax.broadcasted_iota(I32, (ts, LANES), 1) < ts

    def cache_ik(c):
        ik = cik_ref[0, c * sc:(c + 1) * sc, :].astype(BF16)
        return jnp.concatenate([ik, ik], axis=1)

    for c in range(n_chunks):
        key_ref[c] = _sortable(_index_scores(iqs_ref, wrep_ref, cache_ik(c)))
    keyn_ref[...] = jnp.where(new_ok, _sortable(_index_scores(iqs_ref, wrep_ref, pad(ikn_ref[...]))), INT_MIN)

    def count_ge(cand):
        part = jnp.where(keyn_ref[...] >= cand, 1.0, 0.0)
        for c in range(n_chunks):
            part = part + _count_tile(key_ref.at[c], cand)
        return jnp.sum(part, axis=1, keepdims=True)

    t, n_ge = _select_threshold(count_ge, jnp.full((ts, 1), past + ts, F32), topk)
    has_ties = jnp.max(n_ge) > topk

    @pl.when(jnp.logical_not(has_ties))
    def _():
        for c in range(n_chunks):
            key_ref[c] = lax.bitcast_convert_type(_bias_plain(key_ref[c], t), I32)
        keyn_ref[...] = lax.bitcast_convert_type(_bias_plain(keyn_ref[...], t), I32)

    @pl.when(has_ties)
    def _():
        need = topk - count_ge(t + 1)
        run = jnp.zeros((ts, LANES), F32)
        for c in range(n_chunks):
            bias, run = _bias_ties(key_ref[c], t, need, run, tri_ref)
            key_ref[c] = lax.bitcast_convert_type(bias, I32)
        bias, run = _bias_ties(keyn_ref[...], t, need, run, tri_ref)
        keyn_ref[...] = lax.bitcast_convert_type(bias, I32)

    _init_attend(m_ref, acc_ref)
    for c in range(n_chunks):
        _attend_chunk(qs_ref,lax.bitcast_convert_type(key_ref[c], F32),
                      ck_ref[0, c * sc:(c + 1) * sc, :].astype(BF16),
                      cv_ref[0, c * sc:(c + 1) * sc, :].astype(BF16), m_ref, acc_ref)
    _attend_chunk(qs_ref,lax.bitcast_convert_type(keyn_ref[...], F32),
                  pad(kn_ref[...]), pad(vn_ref[...]), m_ref, acc_ref)
    _finish_attend(o_ref, acc_ref)


def _sample_attention(q, iq, iw, kb, vb, ik2, cache_k, cache_v, cache_ik, ts, topk, sc=512):
    t = q.shape[0]
    nb, past = cache_k.shape[:2]
    row = lambda width: pl.BlockSpec((ts, width), lambda b: (b, 0))
    cache = lambda width: pl.BlockSpec((1, past, width), lambda b: (b, 0, 0))
    return pl.pallas_call(
        functools.partial(_sample_attn_kernel, sc=sc, topk=topk),
        out_shape=jax.ShapeDtypeStruct((t, ATTN_DIM), BF16),
        grid=(nb,),
        in_specs=[row(ATTN_DIM), row(ATTN_DIM), row(LANES), row(ATTN_DIM), row(ATTN_DIM), row(LANES),
                  cache(ATTN_DIM), cache(ATTN_DIM), cache(IDX_DIM)],
        out_specs=row(ATTN_DIM),
        scratch_shapes=([pltpu.VMEM((past // sc, ts, sc), I32), pltpu.VMEM((ts, LANES), I32)]
                        + _attn_scratch(ts)),
        compiler_params=pltpu.CompilerParams(dimension_semantics=("arbitrary",),
                                             vmem_limit_bytes=VMEM_LIMIT),
        name="sample_attention",
    )(q, iq, iw, kb, vb, ik2, cache_k, cache_v, cache_ik)


HIST_ROW0 = 2


def _merge_kernel(x_ref, a_ref, hist_ref, g_ref, wglu_ref, wgate_ref, bgate_ref, wa_ref,
                  dww_ref, dwb_ref, lng_ref, lnb_ref, wc_ref, wo_ref, postg_ref,
                  y_ref, conv_o_ref, uext_ref, *, nseq, seqlen, carry):
    i = pl.program_id(0)
    x = x_ref[...]
    n = _rms(x, g_ref[...]).astype(BF16)
    glu = jnp.dot(n, wglu_ref[...], preferred_element_type=F32)
    u = glu[:, :CONV_CH] * _sigmoid(glu[:, CONV_CH:])

    new0 = HIST_ROW0 + HIST
    if carry:
        @pl.when(i == 0)
        def _():
            uext_ref[:, HIST_ROW0:new0, :] = hist_ref[...]

        @pl.when(i > 0)
        def _():
            uext_ref[:, HIST_ROW0:new0, :] = uext_ref[:, seqlen + HIST_ROW0:seqlen + new0, :]
    else:
        uext_ref[:, HIST_ROW0:new0, :] = hist_ref[...]
    uext_ref[:, new0:new0 + seqlen, :] = u.reshape(nseq, seqlen, CONV_CH)
    conv_o_ref[...] = uext_ref[:, seqlen + HIST_ROW0:seqlen + new0, :]

    dww = dww_ref[...]
    cdw = None
    for tap in range(CONV_WIDTH):
        term = uext_ref[:, HIST_ROW0 + tap:HIST_ROW0 + tap + seqlen, :] * dww[tap:tap + 1, :]
        cdw = term if cdw is None else cdw + term
    cdw = cdw.reshape(nseq * seqlen, CONV_CH) + dwb_ref[...]

    mu = jnp.mean(cdw, axis=-1, keepdims=True)
    xc = cdw - mu
    var = jnp.mean(xc * xc, axis=-1, keepdims=True)
    ln = xc * lax.rsqrt(var + EPS) * lng_ref[...] + lnb_ref[...]
    cact = (ln * _sigmoid(ln)).astype(BF16)
    c_proj = jnp.dot(cact, wc_ref[...], preferred_element_type=F32)
    a_proj = jnp.dot(a_ref[...], wa_ref[...], preferred_element_type=F32)

    d = x.shape[1]
    gate = _sigmoid(jnp.dot(n, wgate_ref[...], preferred_element_type=F32) + bgate_ref[...])
    merged = (gate[:, :d] * a_proj + gate[:, d:] * c_proj).astype(BF16)
    y = jnp.dot(merged, wo_ref[...], preferred_element_type=F32)
    y_ref[...] = x + _rms(y, postg_ref[...])


def _branch_merge(x, a, hist, g, wglu, wgate, bgate, wa, dww, dwb, lng, lnb, wc, wo, postg,
                  *, nseq, seqlen, carry):
    t, d = x.shape
    tm = nseq * seqlen
    row = lambda width: pl.BlockSpec((tm, width), lambda i: (i, 0))
    if carry:
        seq_spec = pl.BlockSpec((nseq, HIST, CONV_CH), lambda i: (0, 0, 0))
        n_state = nseq
    else:
        seq_spec = pl.BlockSpec((nseq, HIST, CONV_CH), lambda i: (i, 0, 0))
        n_state = t // seqlen
    consts = [g, wglu, wgate, bgate, wa, dww, dwb, lng, lnb, wc, wo, postg]
    return pl.pallas_call(
        functools.partial(_merge_kernel, nseq=nseq, seqlen=seqlen, carry=carry),
        out_shape=[jax.ShapeDtypeStruct((t, d), F32),
                   jax.ShapeDtypeStruct((n_state, HIST, CONV_CH), F32)],
        grid=(t // tm,),
        in_specs=[row(d), row(ATTN_DIM), seq_spec] + [_const_spec(c.shape) for c in consts],
        out_specs=[row(d), seq_spec],
        scratch_shapes=[pltpu.VMEM((nseq, HIST_ROW0 + HIST + seqlen, CONV_CH), F32)],
        compiler_params=pltpu.CompilerParams(dimension_semantics=("arbitrary",),
                                             vmem_limit_bytes=VMEM_LIMIT),
        name="branch_merge",
    )(x, a, hist, *consts)


def kernel(x_prompt, x_sample, cache_k, cache_v, cache_idx_k, state_conv,
           ffn1_pre_g, ffn1_w_gate, ffn1_w_up, ffn1_w_down, ffn1_post_g,
           mix_pre_g, w_in, b_gate, w_attn_o, conv_dw_w, conv_dw_b, conv_ln_g, conv_ln_b,
           w_conv_o, w_out, mix_post_g,
           ffn2_pre_g, ffn2_w_gate, ffn2_w_up, ffn2_w_down, ffn2_post_g):
    bp, lp, d = x_prompt.shape
    db, ts, _ = x_sample.shape
    depth, _, past = cache_k.shape[:3]
    assert bp == 1 and depth == 1
    topk_p = min(TOPK_MAX, lp // 4)
    topk_s = min(TOPK_MAX, (past + ts) // 4)

    vec = lambda a: a.reshape(1, -1)
    bf = lambda a: a.astype(BF16)
    ffn1 = (vec(ffn1_pre_g[0]), bf(ffn1_w_gate[0]), bf(ffn1_w_up[0]), bf(ffn1_w_down[0]), vec(ffn1_post_g[0]))
    ffn2 = (vec(ffn2_pre_g[0]), bf(ffn2_w_gate[0]), bf(ffn2_w_up[0]), bf(ffn2_w_down[0]), vec(ffn2_post_g[0]))

    widths = (IDX_HEADS * IDX_DIM, IDX_HEADS, IDX_DIM, ATTN_DIM, ATTN_DIM, ATTN_DIM, 2 * CONV_CH, 2 * d)
    offs = [0]
    for wd_ in widths:
        offs.append(offs[-1] + wd_)
    wi = w_in[0]
    col = lambda k: wi[:, offs[k]:offs[k + 1]]
    w_iw = jnp.pad(col(1), ((0, 0), (0, LANES - IDX_HEADS)))
    w2 = bf(jnp.concatenate([col(0), col(3), col(4), col(5), col(2), col(2), w_iw], axis=1))
    merge_w = (vec(mix_pre_g[0]), bf(col(6)), bf(col(7)), vec(b_gate[0]), bf(w_attn_o[0]),
               conv_dw_w[0], vec(conv_dw_b[0]), vec(conv_ln_g[0]), vec(conv_ln_b[0]),
               bf(w_conv_o[0]), bf(w_out[0]), vec(mix_post_g[0]))

    pos_p = jnp.arange(lp, dtype=I32)
    pos_s = jnp.tile(past + jnp.arange(ts, dtype=I32), db)

    hp = _ffn_half(x_prompt.reshape(lp, d), *ffn1)
    iq, q, kf, kb, vf, vb, ikf, ik2, iw = _mixer_proj(hp, vec(mix_pre_g[0]), w2, _rope_angles(pos_p))
    a = _prompt_attention(q, iq, iw, kb, vb, ik2, topk_p)
    hp, conv_p = _branch_merge(hp, a, jnp.zeros((1, HIST, CONV_CH), F32), *merge_w,
                               nseq=1, seqlen=512, carry=True)
    hp = _ffn_half(hp, *ffn2)
    outs_p = (kf.reshape(1, 1, lp, N_HEADS, HEAD_DIM), vf.reshape(1, 1, lp, N_HEADS, HEAD_DIM),
              ikf.reshape(1, 1, lp, IDX_DIM), conv_p.reshape(1, 1, HIST, CONV_CH))

    n_s = db * ts
    hs = _ffn_half(x_sample.reshape(n_s, d), *ffn1)
    iq, q, kf, kb, vf, vb, ikf, ik2, iw = _mixer_proj(hs, vec(mix_pre_g[0]), w2, _rope_angles(pos_s))
    a = _sample_attention(q, iq, iw, kb, vb, ik2, cache_k[0].reshape(db, past, ATTN_DIM),
                          cache_v[0].reshape(db, past, ATTN_DIM), cache_idx_k[0], ts, topk_s)
    hs, conv_s = _branch_merge(hs, a, state_conv[0], *merge_w, nseq=8, seqlen=ts, carry=False)
    hs = _ffn_half(hs, *ffn2)
    outs_s = (kf.reshape(1, db, ts, N_HEADS, HEAD_DIM), vf.reshape(1, db, ts, N_HEADS, HEAD_DIM),
              ikf.reshape(1, db, ts, IDX_DIM), conv_s.reshape(1, db, HIST, CONV_CH))

    return (hp.reshape(1, lp, d), hs.reshape(db, ts, d)) + outs_p + outs_s
```

```python
import functools
import math

import jax
import jax.numpy as jnp
from jax import lax
from jax.experimental import pallas as pl
from jax.experimental.pallas import tpu as pltpu

F32 = jnp.float32
BF16 = jnp.bfloat16
I32 = jnp.int32

CHUNK = 64
N_HEADS = 8
HEAD_DIM = 64
ATTN_DIM = N_HEADS * HEAD_DIM
IDX_HEADS = 8
IDX_DIM = 64
IDX_W_SCALE = (IDX_HEADS * IDX_DIM) ** -0.5
TOPK_MAX = 256
ROPE_THETA = 500000.0
ROT = HEAD_DIM // 4
CONV_CH = 512
CONV_WIDTH = 31
HIST = CONV_WIDTH - 1
EPS = 1e-6

LANES = 128
VMEM_LIMIT = 56 * 1024 * 1024
INT_MIN = -(2 ** 31)
NEG_BIG = -1e30
Q_SCALE = HEAD_DIM ** -0.5 * math.log2(math.e)

_NT = (((1,), (1,)), ((), ()))


def _rms(x, g):
    return x * lax.rsqrt(jnp.mean(x * x, axis=-1, keepdims=True) + EPS) * g


def _sigmoid(x):
    return 1.0 / (1.0 + jnp.exp(-x))


def _const_spec(shape):
    nd = len(shape)
    return pl.BlockSpec(shape, lambda *_: (0,) * nd, pipeline_mode=pl.Buffered(1))


def _ffn_kernel(x_ref, pre_g_ref, wg_ref, wu_ref, wd_ref, post_g_ref, o_ref):
    x = x_ref[...]
    h = _rms(x, pre_g_ref[...]).astype(BF16)
    g = jnp.dot(h, wg_ref[...], preferred_element_type=F32)
    u = jnp.dot(h, wu_ref[...], preferred_element_type=F32)
    a = (g * _sigmoid(g) * u).astype(BF16)
    f = jnp.dot(a, wd_ref[...], preferred_element_type=F32)
    o_ref[...] = x + 0.5 * _rms(f, post_g_ref[...])


def _ffn_half(x, pre_g, wg, wu, wd, post_g, tm=512):
    t, d = x.shape
    dff = wg.shape[1]
    row = pl.BlockSpec((tm, d), lambda i: (i, 0))
    return pl.pallas_call(
        _ffn_kernel,
        out_shape=jax.ShapeDtypeStruct((t, d), F32),
        grid=(t // tm,),
        in_specs=[row, _const_spec((1, d)), _const_spec((d, dff)), _const_spec((d, dff)),
                  _const_spec((dff, d)), _const_spec((1, d))],
        out_specs=row,
        compiler_params=pltpu.CompilerParams(dimension_semantics=("parallel",),
                                             vmem_limit_bytes=VMEM_LIMIT),
        name="ffn_half",
    )(x, pre_g, wg, wu, wd, post_g)


def _rope_angles(pos):
    half = ROT // 2
    inv_freq = jnp.exp(-math.log(ROPE_THETA) * jnp.arange(half, dtype=F32) * (2.0 / ROT))
    ang = pos.astype(F32)[:, None] * inv_freq[None, :]
    return jnp.pad(jnp.concatenate([jnp.cos(ang), jnp.sin(ang)], axis=1), ((0, 0), (0, LANES - ROT)))


def _rope_tables(cs):
    half = ROT // 2
    lane = lax.broadcasted_iota(I32, cs.shape, 1)
    both = lambda a: a + pltpu.roll(a, HEAD_DIM, 1)
    cos_lo = jnp.where(lane < half, cs, 0.0)
    sin_hi = jnp.where(jnp.logical_and(lane >= half, lane < ROT), cs, 0.0)
    c = both(cos_lo + pltpu.roll(cos_lo, half, 1))
    c = jnp.where(lane % HEAD_DIM < ROT, c, 1.0)
    s_dn = both(sin_hi)
    s_up = -both(pltpu.roll(sin_hi, LANES - half, 1))
    return c, s_up, s_dn


def _proj_kernel(x_ref, g_ref, w_ref, cs_ref,
                 iq_o, q_o, kf_o, kb_o, vf_o, vb_o, ikf_o, ik2_o, iw_o):
    n = _rms(x_ref[...], g_ref[...]).astype(BF16)
    p = jnp.dot(n, w_ref[...], preferred_element_type=F32)
    c, s_up, s_dn = _rope_tables(cs_ref[...])

    def rope(x):
        return x * c + pltpu.roll(x, LANES - ROT // 2, 1) * s_up + pltpu.roll(x, ROT // 2, 1) * s_dn

    a = ATTN_DIM
    for gidx in range(a // LANES):
        sl = slice(gidx * LANES, (gidx + 1) * LANES)
        iq_o[:, sl] = rope(p[:, sl]).astype(BF16)
        q_o[:, sl] = (rope(p[:, a + gidx * LANES:a + (gidx + 1) * LANES]) * Q_SCALE).astype(BF16)
        k = rope(p[:, 2 * a + gidx * LANES:2 * a + (gidx + 1) * LANES])
        kf_o[:, sl] = k
        kb_o[:, sl] = k.astype(BF16)
        v = p[:, 3 * a + gidx * LANES:3 * a + (gidx + 1) * LANES]
        vf_o[:, sl] = v
        vb_o[:, sl] = v.astype(BF16)
    ik = rope(p[:, 4 * a:4 * a + LANES])
    ikf_o[...] = ik[:, :IDX_DIM]
    ik2_o[...] = ik.astype(BF16)
    iw_o[...] = p[:, 4 * a + LANES:4 * a + 2 * LANES] * IDX_W_SCALE


def _mixer_proj(x, g, w2, angles, tm=512):
    t, d = x.shape
    ncol = w2.shape[1]
    row = lambda width: pl.BlockSpec((tm, width), lambda i: (i, 0))
    a = ATTN_DIM
    outs = [
        jax.ShapeDtypeStruct((t, a), BF16),
        jax.ShapeDtypeStruct((t, a), BF16),
        jax.ShapeDtypeStruct((t, a), F32),
        jax.ShapeDtypeStruct((t, a), BF16),
        jax.ShapeDtypeStruct((t, a), F32),
        jax.ShapeDtypeStruct((t, a), BF16),
        jax.ShapeDtypeStruct((t, IDX_DIM), F32),
        jax.ShapeDtypeStruct((t, LANES), BF16),
        jax.ShapeDtypeStruct((t, LANES), F32),
    ]
    return pl.pallas_call(
        _proj_kernel,
        out_shape=outs,
        grid=(t // tm,),
        in_specs=[row(d), _const_spec((1, d)), _const_spec((d, ncol)), row(LANES)],
        out_specs=[row(a), row(a), row(a), row(a), row(a), row(a), row(IDX_DIM), row(LANES), row(LANES)],
        compiler_params=pltpu.CompilerParams(dimension_semantics=("parallel",),
                                             vmem_limit_bytes=VMEM_LIMIT),
        name="mixer_proj",
    )(x, g, w2, angles)


def _sortable(x):
    b = lax.bitcast_convert_type(x, I32)
    k = b ^ ((b >> 31) & 0x7FFFFFFF)
    return jnp.where(x == 0.0, 0, k)


def _head_lhs(x, h):
    grp, odd = divmod(h, 2)
    xg = x[:, grp * LANES:(grp + 1) * LANES].astype(F32)
    lane = lax.broadcasted_iota(I32, xg.shape, 1)
    keep = (lane >= HEAD_DIM) if odd else (lane < HEAD_DIM)
    return jnp.where(keep, xg, 0.0).astype(BF16)


def _stack_heads(x_ref, dst_ref, rows):
    x = x_ref[...]
    for h in range(N_HEADS):
        dst_ref[h * rows:(h + 1) * rows, :] = _head_lhs(x, h)


def _spread_weights(w, wrep_ref):
    for h in range(IDX_HEADS):
        wrep_ref[h] = jnp.broadcast_to(w[:, h:h + 1], wrep_ref.shape[1:])


def _index_dots(iqs_ref, ikc):
    return lax.dot_general(iqs_ref[...], ikc, _NT, preferred_element_type=F32)


def _index_combine(d, wrep_ref):
    rows = wrep_ref.shape[1]
    groups = []
    for l in range(d.shape[1] // LANES):
        acc = None
        for h in range(IDX_HEADS):
            term = jnp.maximum(d[h * rows:(h + 1) * rows, l * LANES:(l + 1) * LANES], 0.0) * wrep_ref[h]
            acc = term if acc is None else acc + term
        groups.append(acc)
    return jnp.concatenate(groups, axis=1)


def _index_scores(iqs_ref, wrep_ref, ikc):
    return _index_combine(_index_dots(iqs_ref, ikc), wrep_ref)


def _lane_fold(x):
    out = x[:, :LANES]
    for l in range(1, x.shape[1] // LANES):
        out = out + x[:, l * LANES:(l + 1) * LANES]
    return out


COUNT_ROWS = 32


def _count_tile(tile_ref, cand):
    rows = tile_ref.shape[0]
    step = min(COUNT_ROWS, rows)
    parts = []
    for r in range(0, rows, step):
        hit = jnp.where(tile_ref[r:r + step, :] >= cand[r:r + step], 1.0, 0.0)
        parts.append(_lane_fold(hit))
    return jnp.concatenate(parts, axis=0)


def _select_threshold(count_ge, n_adm, topk):
    c0 = count_ge(jnp.zeros(n_adm.shape, I32))
    c1 = count_ge(jnp.ones(n_adm.shape, I32))
    nonneg = c0 >= topk
    zero_tie = jnp.logical_and(nonneg, c1 < topk)

    def cond(st):
        i, _, cnt = st
        return jnp.logical_and(i < 32, jnp.max(jnp.where(zero_tie, 0.0, cnt)) > topk)

    def one_bit(i, t, cnt):
        cand = t + jnp.left_shift(jnp.int32(1), 31 - i)
        c = count_ge(cand)
        ok = c >= topk
        return jnp.where(ok, cand, t), jnp.where(ok, c, cnt)

    def step(st):
        i, t, cnt = st
        t, cnt = one_bit(i, t, cnt)
        t, cnt = one_bit(i + 1, t, cnt)
        return i + 2, t, cnt

    t, cnt = one_bit(1, jnp.where(nonneg, 0, INT_MIN), jnp.where(nonneg, c0, n_adm))
    _, t, cnt = lax.while_loop(cond, step, (jnp.int32(2), t, cnt))
    return jnp.maximum(t, INT_MIN + 1), cnt


def _bias_plain(key, t):
    return jnp.where(key >= t, 0.0, NEG_BIG)


def _init_tri(tri_ref):
    @pl.when(pl.program_id(0) == 0)
    def _():
        r = lax.broadcasted_iota(I32, tri_ref.shape, 0)
        c = lax.broadcasted_iota(I32, tri_ref.shape, 1)
        tri_ref[...] = jnp.where(jnp.logical_or(r < c, c >= LANES), 1.0, 0.0).astype(BF16)


def _tie_counts(key, t, tri_ref):
    eq = key == t
    eqb = jnp.where(eq, 1.0, 0.0).astype(BF16)
    stacked = jnp.concatenate([eqb[:, l * LANES:(l + 1) * LANES] for l in range(key.shape[1] // LANES)], axis=0)
    return eq, jnp.dot(stacked, tri_ref[...], preferred_element_type=F32)


def _tie_bias(key, t, eq, counts, need, run):
    rows, sc = key.shape
    before = []
    for l in range(sc // LANES):
        blk = counts[l * rows:(l + 1) * rows]
        before.append(run + blk[:, :LANES])
        run = run + blk[:, LANES:]
    sel = (key > t) | (eq & (jnp.concatenate(before, axis=1) < need))
    return jnp.where(sel, 0.0, NEG_BIG), run


def _bias_ties(key, t, need, run, tri_ref):
    eq, counts = _tie_counts(key, t, tri_ref)
    return _tie_bias(key, t, eq, counts, need, run)


def _pair_scores(qs_ref, kc):
    pairs = N_HEADS // 2
    prow = qs_ref.shape[0] // pairs
    return [lax.dot_general(qs_ref[g * prow:(g + 1) * prow, :], kc[:, g * LANES:(g + 1) * LANES], _NT,
                            preferred_element_type=F32) for g in range(pairs)]


def _softmax_pv(scores, bias, vc, m_ref, acc_ref):
    pairs = N_HEADS // 2
    nrep = vc.shape[0] // LANES
    bias2 = jnp.concatenate([bias, bias], axis=0)
    ones = jnp.ones((vc.shape[0], LANES), BF16)
    probs, alphas = [], []
    for g in range(pairs):
        s = scores[g] + bias2
        m_prev = m_ref[g]
        m_next = jnp.maximum(m_prev, jnp.max(s, axis=1, keepdims=True))
        alphas.append(jnp.exp2(m_prev - m_next))
        probs.append(jnp.exp2(s - jnp.concatenate([m_next] * nrep, axis=1)).astype(BF16))
        m_ref[g] = m_next
    for g in range(pairs):
        vext = jnp.concatenate([vc[:, g * LANES:(g + 1) * LANES], ones], axis=1)
        pv = jnp.dot(probs[g], vext, preferred_element_type=F32)
        acc_ref[g] = jnp.concatenate([alphas[g], alphas[g]], axis=1) * acc_ref[g] + pv


def _attend_chunk(qs_ref, bias, kc, vc, m_ref, acc_ref):
    _softmax_pv(_pair_scores(qs_ref, kc), bias, vc, m_ref, acc_ref)


def _init_attend(m_ref, acc_ref):
    m_ref[...] = jnp.full(m_ref.shape, NEG_BIG, F32)
    acc_ref[...] = jnp.zeros(acc_ref.shape, F32)


def _finish_attend(o_ref, acc_ref):
    rows = acc_ref.shape[1] // 2
    lane = lax.broadcasted_iota(I32, (rows, LANES), 1)
    for g in range(N_HEADS // 2):
        acc = acc_ref[g]
        o = acc[:, :LANES] / acc[:, LANES:]
        o_ref[:, g * LANES:(g + 1) * LANES] = jnp.where(lane < HEAD_DIM, o[:rows], o[rows:]).astype(o_ref.dtype)


def _prompt_attn_kernel(q_ref, iq_ref, iw_ref, kb_ref, vb_ref, ik2_ref, o_ref,
                        key_ref, iqs_ref, qs_ref, wrep_ref, m_ref, acc_ref, tri_ref, *, tq, sc, topk):
    j = pl.program_id(0)
    n_chunks = ((j + 1) * tq + sc - 1) // sc

    _init_tri(tri_ref)
    _stack_heads(iq_ref, iqs_ref, tq)
    _stack_heads(q_ref, qs_ref, tq)
    _spread_weights(iw_ref[...], wrep_ref)
    row = lax.broadcasted_iota(I32, (tq, 1), 0)
    limit = j * tq + (row // CHUNK + 1) * CHUNK

    def pairs_then_tail(fn, carry=0):
        carry = lax.fori_loop(0, n_chunks // 2, lambda i, carry: fn((2 * i, 2 * i + 1), carry), carry)

        @pl.when(n_chunks % 2 == 1)
        def _():
            fn((n_chunks - 1,), carry)

    def score_chunks(chunks, carry):
        offs = [pl.multiple_of(c * sc, sc) for c in chunks]
        dots = [_index_dots(iqs_ref, ik2_ref[pl.ds(off, sc), :]) for off in offs]
        for c, off, d in zip(chunks, offs, dots):
            sidx = off + lax.broadcasted_iota(I32, (tq, sc), 1)
            key_ref[c] = jnp.where(sidx < limit, _sortable(_index_combine(d, wrep_ref)), INT_MIN)
        return carry
    pairs_then_tail(score_chunks)

    def count_ge(cand):
        def body(c, acc):
            return acc + _count_tile(key_ref.at[c], cand)
        part = lax.fori_loop(0, n_chunks, body, jnp.zeros((tq, LANES), F32))
        return jnp.sum(part, axis=1, keepdims=True)

    t, n_ge = _select_threshold(count_ge, limit.astype(F32), topk)
    has_ties = jnp.max(n_ge) > topk

    @pl.when(jnp.logical_not(has_ties))
    def _():
        def body(c, carry):
            key_ref[c] = lax.bitcast_convert_type(_bias_plain(key_ref[c], t), I32)
            return carry
        lax.fori_loop(0, n_chunks, body, 0)

    @pl.when(has_ties)
    def _():
        need = topk - count_ge(t + 1)

        def tie_chunks(chunks, run):
            keys = [key_ref[c] for c in chunks]
            counted = [_tie_counts(key, t, tri_ref) for key in keys]
            for c, key, (eq, counts) in zip(chunks, keys, counted):
                bias, run = _tie_bias(key, t, eq, counts, need, run)
                key_ref[c] = lax.bitcast_convert_type(bias, I32)
            return run
        pairs_then_tail(tie_chunks, jnp.zeros((tq, LANES), F32))

    _init_attend(m_ref, acc_ref)

    def attend_chunks(chunks, carry):
        offs = [pl.multiple_of(c * sc, sc) for c in chunks]
        scores = [_pair_scores(qs_ref, kb_ref[pl.ds(off, sc), :]) for off in offs]
        for c, off, s in zip(chunks, offs, scores):
            _softmax_pv(s, lax.bitcast_convert_type(key_ref[c], F32), vb_ref[pl.ds(off, sc), :], m_ref, acc_ref)
        return carry
    pairs_then_tail(attend_chunks)
    _finish_attend(o_ref, acc_ref)


def _attn_scratch(rows):
    pairs = N_HEADS // 2
    return [
        pltpu.VMEM((N_HEADS * rows, LANES), BF16),
        pltpu.VMEM((N_HEADS * rows, LANES), BF16),
        pltpu.VMEM((IDX_HEADS, rows, LANES), F32),
        pltpu.VMEM((pairs, 2 * rows, LANES), F32),
        pltpu.VMEM((pairs, 2 * rows, 2 * LANES), F32),
        pltpu.VMEM((LANES, 2 * LANES), BF16),
    ]


def _prompt_attention(q, iq, iw, kb, vb, ik2, topk, tq=128, sc=512):
    s = kb.shape[0]
    row = lambda width: pl.BlockSpec((tq, width), lambda j: (j, 0))
    whole = pl.BlockSpec(memory_space=pltpu.VMEM)
    return pl.pallas_call(
        functools.partial(_prompt_attn_kernel, tq=tq, sc=sc, topk=topk),
        out_shape=jax.ShapeDtypeStruct((s, ATTN_DIM), BF16),
        grid=(s // tq,),
        in_specs=[row(ATTN_DIM), row(ATTN_DIM), row(LANES), whole, whole, whole],
        out_specs=row(ATTN_DIM),
        scratch_shapes=[pltpu.VMEM((s // sc, tq, sc), I32)] + _attn_scratch(tq),
        compiler_params=pltpu.CompilerParams(dimension_semantics=("arbitrary",),
                                             vmem_limit_bytes=VMEM_LIMIT),
        name="prompt_attention",
    )(q, iq, iw, kb, vb, ik2)


def _sample_attn_kernel(q_ref, iq_ref, iw_ref, kn_ref, vn_ref, ikn_ref, ck_ref, cv_ref, cik_ref, o_ref,
                        key_ref, keyn_ref, iqs_ref, qs_ref, wrep_ref, m_ref, acc_ref, tri_ref, *, sc, topk):
    ts = q_ref.shape[0]
    past = ck_ref.shape[1]
    n_chunks = past // sc

    _init_tri(tri_ref)
    _stack_heads(iq_ref, iqs_ref, ts)
    _stack_heads(q_ref, qs_ref, ts)
    _spread_weights(iw_ref[...], wrep_ref)

    pad = lambda x: jnp.concatenate([x, jnp.zeros((LANES - ts, x.shape[1]), x.dtype)], axis=0)
    new_ok = lax.broadcasted_iota(I32, (ts, LANES), 1) < ts

    def cache_ik(c):
        ik = cik_ref[0, c * sc:(c + 1) * sc, :].astype(BF16)
        return jnp.concatenate([ik, ik], axis=1)

    for c in range(n_chunks):
        key_ref[c] = _sortable(_index_scores(iqs_ref, wrep_ref, cache_ik(c)))
    keyn_ref[...] = jnp.where(new_ok, _sortable(_index_scores(iqs_ref, wrep_ref, pad(ikn_ref[...]))), INT_MIN)

    def count_ge(cand):
        part = jnp.where(keyn_ref[...] >= cand, 1.0, 0.0)
        for c in range(n_chunks):
            part = part + _count_tile(key_ref.at[c], cand)
        return jnp.sum(part, axis=1, keepdims=True)

    t, n_ge = _select_threshold(count_ge, jnp.full((ts, 1), past + ts, F32), topk)
    has_ties = jnp.max(n_ge) > topk

    @pl.when(jnp.logical_not(has_ties))
    def _():
        for c in range(n_chunks):
            key_ref[c] = lax.bitcast_convert_type(_bias_plain(key_ref[c], t), I32)
        keyn_ref[...] = lax.bitcast_convert_type(_bias_plain(keyn_ref[...], t), I32)

    @pl.when(has_ties)
    def _():
        need = topk - count_ge(t + 1)
        run = jnp.zeros((ts, LANES), F32)
        for c in range(n_chunks):
            bias, run = _bias_ties(key_ref[c], t, need, run, tri_ref)
            key_ref[c] = lax.bitcast_convert_type(bias, I32)
        bias, run = _bias_ties(keyn_ref[...], t, need, run, tri_ref)
        keyn_ref[...] = lax.bitcast_convert_type(bias, I32)

    _init_attend(m_ref, acc_ref)
    for c in range(n_chunks):
        _attend_chunk(qs_ref,lax.bitcast_convert_type(key_ref[c], F32),
                      ck_ref[0, c * sc:(c + 1) * sc, :].astype(BF16),
                      cv_ref[0, c * sc:(c + 1) * sc, :].astype(BF16), m_ref, acc_ref)
    _attend_chunk(qs_ref,lax.bitcast_convert_type(keyn_ref[...], F32),
                  pad(kn_ref[...]), pad(vn_ref[...]), m_ref, acc_ref)
    _finish_attend(o_ref, acc_ref)


def _sample_attention(q, iq, iw, kb, vb, ik2, cache_k, cache_v, cache_ik, ts, topk, sc=512):
    t = q.shape[0]
    nb, past = cache_k.shape[:2]
    row = lambda width: pl.BlockSpec((ts, width), lambda b: (b, 0))
    cache = lambda width: pl.BlockSpec((1, past, width), lambda b: (b, 0, 0))
    return pl.pallas_call(
        functools.partial(_sample_attn_kernel, sc=sc, topk=topk),
        out_shape=jax.ShapeDtypeStruct((t, ATTN_DIM), BF16),
        grid=(nb,),
        in_specs=[row(ATTN_DIM), row(ATTN_DIM), row(LANES), row(ATTN_DIM), row(ATTN_DIM), row(LANES),
                  cache(ATTN_DIM), cache(ATTN_DIM), cache(IDX_DIM)],
        out_specs=row(ATTN_DIM),
        scratch_shapes=([pltpu.VMEM((past // sc, ts, sc), I32), pltpu.VMEM((ts, LANES), I32)]
                        + _attn_scratch(ts)),
        compiler_params=pltpu.CompilerParams(dimension_semantics=("arbitrary",),
                                             vmem_limit_bytes=VMEM_LIMIT),
        name="sample_attention",
    )(q, iq, iw, kb, vb, ik2, cache_k, cache_v, cache_ik)


HIST_ROW0 = 2


def _merge_kernel(x_ref, a_ref, hist_ref, g_ref, wglu_ref, wgate_ref, bgate_ref, wa_ref,
                  dww_ref, dwb_ref, lng_ref, lnb_ref, wc_ref, wo_ref, postg_ref,
                  y_ref, conv_o_ref, uext_ref, *, nseq, seqlen, carry):
    i = pl.program_id(0)
    x = x_ref[...]
    n = _rms(x, g_ref[...]).astype(BF16)
    glu = jnp.dot(n, wglu_ref[...], preferred_element_type=F32)
    u = glu[:, :CONV_CH] * _sigmoid(glu[:, CONV_CH:])

    new0 = HIST_ROW0 + HIST
    if carry:
        @pl.when(i == 0)
        def _():
            uext_ref[:, HIST_ROW0:new0, :] = hist_ref[...]

        @pl.when(i > 0)
        def _():
            uext_ref[:, HIST_ROW0:new0, :] = uext_ref[:, seqlen + HIST_ROW0:seqlen + new0, :]
    else:
        uext_ref[:, HIST_ROW0:new0, :] = hist_ref[...]
    uext_ref[:, new0:new0 + seqlen, :] = u.reshape(nseq, seqlen, CONV_CH)
    conv_o_ref[...] = uext_ref[:, seqlen + HIST_ROW0:seqlen + new0, :]

    dww = dww_ref[...]
    cdw = None
    for tap in range(CONV_WIDTH):
        term = uext_ref[:, HIST_ROW0 + tap:HIST_ROW0 + tap + seqlen, :] * dww[tap:tap + 1, :]
        cdw = term if cdw is None else cdw + term
    cdw = cdw.reshape(nseq * seqlen, CONV_CH) + dwb_ref[...]

    mu = jnp.mean(cdw, axis=-1, keepdims=True)
    xc = cdw - mu
    var = jnp.mean(xc * xc, axis=-1, keepdims=True)
    ln = xc * lax.rsqrt(var + EPS) * lng_ref[...] + lnb_ref[...]
    cact = (ln * _sigmoid(ln)).astype(BF16)
    c_proj = jnp.dot(cact, wc_ref[...], preferred_element_type=F32)
    a_proj = jnp.dot(a_ref[...], wa_ref[...], preferred_element_type=F32)

    d = x.shape[1]
    gate = _sigmoid(jnp.dot(n, wgate_ref[...], preferred_element_type=F32) + bgate_ref[...])
    merged = (gate[:, :d] * a_proj + gate[:, d:] * c_proj).astype(BF16)
    y = jnp.dot(merged, wo_ref[...], preferred_element_type=F32)
    y_ref[...] = x + _rms(y, postg_ref[...])


def _branch_merge(x, a, hist, g, wglu, wgate, bgate, wa, dww, dwb, lng, lnb, wc, wo, postg,
                  *, nseq, seqlen, carry):
    t, d = x.shape
    tm = nseq * seqlen
    row = lambda width: pl.BlockSpec((tm, width), lambda i: (i, 0))
    if carry:
        seq_spec = pl.BlockSpec((nseq, HIST, CONV_CH), lambda i: (0, 0, 0))
        n_state = nseq
    else:
        seq_spec = pl.BlockSpec((nseq, HIST, CONV_CH), lambda i: (i, 0, 0))
        n_state = t // seqlen
    consts = [g, wglu, wgate, bgate, wa, dww, dwb, lng, lnb, wc, wo, postg]
    return pl.pallas_call(
        functools.partial(_merge_kernel, nseq=nseq, seqlen=seqlen, carry=carry),
        out_shape=[jax.ShapeDtypeStruct((t, d), F32),
                   jax.ShapeDtypeStruct((n_state, HIST, CONV_CH), F32)],
        grid=(t // tm,),
        in_specs=[row(d), row(ATTN_DIM), seq_spec] + [_const_spec(c.shape) for c in consts],
        out_specs=[row(d), seq_spec],
        scratch_shapes=[pltpu.VMEM((nseq, HIST_ROW0 + HIST + seqlen, CONV_CH), F32)],
        compiler_params=pltpu.CompilerParams(dimension_semantics=("arbitrary",),
                                             vmem_limit_bytes=VMEM_LIMIT),
        name="branch_merge",
    )(x, a, hist, *consts)


def kernel(x_prompt, x_sample, cache_k, cache_v, cache_idx_k, state_conv,
           ffn1_pre_g, ffn1_w_gate, ffn1_w_up, ffn1_w_down, ffn1_post_g,
           mix_pre_g, w_in, b_gate, w_attn_o, conv_dw_w, conv_dw_b, conv_ln_g, conv_ln_b,
           w_conv_o, w_out, mix_post_g,
           ffn2_pre_g, ffn2_w_gate, ffn2_w_up, ffn2_w_down, ffn2_post_g):
    bp, lp, d = x_prompt.shape
    db, ts, _ = x_sample.shape
    depth, _, past = cache_k.shape[:3]
    assert bp == 1 and depth == 1
    topk_p = min(TOPK_MAX, lp // 4)
    topk_s = min(TOPK_MAX, (past + ts) // 4)

    vec = lambda a: a.reshape(1, -1)
    bf = lambda a: a.astype(BF16)
    ffn1 = (vec(ffn1_pre_g[0]), bf(ffn1_w_gate[0]), bf(ffn1_w_up[0]), bf(ffn1_w_down[0]), vec(ffn1_post_g[0]))
    ffn2 = (vec(ffn2_pre_g[0]), bf(ffn2_w_gate[0]), bf(ffn2_w_up[0]), bf(ffn2_w_down[0]), vec(ffn2_post_g[0]))

    widths = (IDX_HEADS * IDX_DIM, IDX_HEADS, IDX_DIM, ATTN_DIM, ATTN_DIM, ATTN_DIM, 2 * CONV_CH, 2 * d)
    offs = [0]
    for wd_ in widths:
        offs.append(offs[-1] + wd_)
    wi = w_in[0]
    col = lambda k: wi[:, offs[k]:offs[k + 1]]
    w_iw = jnp.pad(col(1), ((0, 0), (0, LANES - IDX_HEADS)))
    w2 = bf(jnp.concatenate([col(0), col(3), col(4), col(5), col(2), col(2), w_iw], axis=1))
    merge_w = (vec(mix_pre_g[0]), bf(col(6)), bf(col(7)), vec(b_gate[0]), bf(w_attn_o[0]),
               conv_dw_w[0], vec(conv_dw_b[0]), vec(conv_ln_g[0]), vec(conv_ln_b[0]),
               bf(w_conv_o[0]), bf(w_out[0]), vec(mix_post_g[0]))

    pos_p = jnp.arange(lp, dtype=I32)
    pos_s = jnp.tile(past + jnp.arange(ts, dtype=I32), db)

    hp = _ffn_half(x_prompt.reshape(lp, d), *ffn1)
    iq, q, kf, kb, vf, vb, ikf, ik2, iw = _mixer_proj(hp, vec(mix_pre_g[0]), w2, _rope_angles(pos_p))
    a = _prompt_attention(q, iq, iw, kb, vb, ik2, topk_p)
    hp, conv_p = _branch_merge(hp, a, jnp.zeros((1, HIST, CONV_CH), F32), *merge_w,
                               nseq=1, seqlen=512, carry=True)
    hp = _ffn_half(hp, *ffn2)
    outs_p = (kf.reshape(1, 1, lp, N_HEADS, HEAD_DIM), vf.reshape(1, 1, lp, N_HEADS, HEAD_DIM),
              ikf.reshape(1, 1, lp, IDX_DIM), conv_p.reshape(1, 1, HIST, CONV_CH))

    n_s = db * ts
    hs = _ffn_half(x_sample.reshape(n_s, d), *ffn1)
    iq, q, kf, kb, vf, vb, ikf, ik2, iw = _mixer_proj(hs, vec(mix_pre_g[0]), w2, _rope_angles(pos_s))
    a = _sample_attention(q, iq, iw, kb, vb, ik2, cache_k[0].reshape(db, past, ATTN_DIM),
                          cache_v[0].reshape(db, past, ATTN_DIM), cache_idx_k[0], ts, topk_s)
    hs, conv_s = _branch_merge(hs, a, state_conv[0], *merge_w, nseq=8, seqlen=ts, carry=False)
    hs = _ffn_half(hs, *ffn2)
    outs_s = (kf.reshape(1, db, ts, N_HEADS, HEAD_DIM), vf.reshape(1, db, ts, N_HEADS, HEAD_DIM),
              ikf.reshape(1, db, ts, IDX_DIM), conv_s.reshape(1, db, HIST, CONV_CH))

    return (hp.reshape(1, lp, d), hs.reshape(db, ts, d)) + outs_p + outs_s
```

```python
import functools
import math

import jax
import jax.numpy as jnp
from jax import lax
from jax.experimental import pallas as pl
from jax.experimental.pallas import tpu as pltpu

F32 = jnp.float32
BF16 = jnp.bfloat16
I32 = jnp.int32

CHUNK = 64
N_HEADS = 8
HEAD_DIM = 64
ATTN_DIM = N_HEADS * HEAD_DIM
IDX_HEADS = 8
IDX_DIM = 64
IDX_W_SCALE = (IDX_HEADS * IDX_DIM) ** -0.5
TOPK_MAX = 256
ROPE_THETA = 500000.0
ROT = HEAD_DIM // 4
CONV_CH = 512
CONV_WIDTH = 31
HIST = CONV_WIDTH - 1
EPS = 1e-6

LANES = 128
VMEM_LIMIT = 56 * 1024 * 1024
INT_MIN = -(2 ** 31)
NEG_BIG = -1e30
Q_SCALE = HEAD_DIM ** -0.5 * math.log2(math.e)

_NT = (((1,), (1,)), ((), ()))


def _rms(x, g):
    return x * lax.rsqrt(jnp.mean(x * x, axis=-1, keepdims=True) + EPS) * g


def _sigmoid(x):
    return 1.0 / (1.0 + jnp.exp(-x))


def _const_spec(shape):
    nd = len(shape)
    return pl.BlockSpec(shape, lambda *_: (0,) * nd, pipeline_mode=pl.Buffered(1))


def _ffn_kernel(x_ref, pre_g_ref, wg_ref, wu_ref, wd_ref, post_g_ref, o_ref):
    x = x_ref[...]
    h = _rms(x, pre_g_ref[...]).astype(BF16)
    g = jnp.dot(h, wg_ref[...], preferred_element_type=F32)
    u = jnp.dot(h, wu_ref[...], preferred_element_type=F32)
    a = (g * _sigmoid(g) * u).astype(BF16)
    f = jnp.dot(a, wd_ref[...], preferred_element_type=F32)
    o_ref[...] = x + 0.5 * _rms(f, post_g_ref[...])


def _ffn_half(x, pre_g, wg, wu, wd, post_g, tm=512):
    t, d = x.shape
    dff = wg.shape[1]
    row = pl.BlockSpec((tm, d), lambda i: (i, 0))
    return pl.pallas_call(
        _ffn_kernel,
        out_shape=jax.ShapeDtypeStruct((t, d), F32),
        grid=(t // tm,),
        in_specs=[row, _const_spec((1, d)), _const_spec((d, dff)), _const_spec((d, dff)),
                  _const_spec((dff, d)), _const_spec((1, d))],
        out_specs=row,
        compiler_params=pltpu.CompilerParams(dimension_semantics=("parallel",),
                                             vmem_limit_bytes=VMEM_LIMIT),
        name="ffn_half",
    )(x, pre_g, wg, wu, wd, post_g)


def _rope_angles(pos):
    half = ROT // 2
    inv_freq = jnp.exp(-math.log(ROPE_THETA) * jnp.arange(half, dtype=F32) * (2.0 / ROT))
    ang = pos.astype(F32)[:, None] * inv_freq[None, :]
    return jnp.pad(jnp.concatenate([jnp.cos(ang), jnp.sin(ang)], axis=1), ((0, 0), (0, LANES - ROT)))


def _rope_tables(cs):
    half = ROT // 2
    lane = lax.broadcasted_iota(I32, cs.shape, 1)
    both = lambda a: a + pltpu.roll(a, HEAD_DIM, 1)
    cos_lo = jnp.where(lane < half, cs, 0.0)
    sin_hi = jnp.where(jnp.logical_and(lane >= half, lane < ROT), cs, 0.0)
    c = both(cos_lo + pltpu.roll(cos_lo, half, 1))
    c = jnp.where(lane % HEAD_DIM < ROT, c, 1.0)
    s_dn = both(sin_hi)
    s_up = -both(pltpu.roll(sin_hi, LANES - half, 1))
    return c, s_up, s_dn


def _proj_kernel(x_ref, g_ref, w_ref, cs_ref,
                 iq_o, q_o, kf_o, kb_o, vf_o, vb_o, ikf_o, ik2_o, iw_o):
    n = _rms(x_ref[...], g_ref[...]).astype(BF16)
    p = jnp.dot(n, w_ref[...], preferred_element_type=F32)
    c, s_up, s_dn = _rope_tables(cs_ref[...])

    def rope(x):
        return x * c + pltpu.roll(x, LANES - ROT // 2, 1) * s_up + pltpu.roll(x, ROT // 2, 1) * s_dn

    a = ATTN_DIM
    for gidx in range(a // LANES):
        sl = slice(gidx * LANES, (gidx + 1) * LANES)
        iq_o[:, sl] = rope(p[:, sl]).astype(BF16)
        q_o[:, sl] = (rope(p[:, a + gidx * LANES:a + (gidx + 1) * LANES]) * Q_SCALE).astype(BF16)
        k = rope(p[:, 2 * a + gidx * LANES:2 * a + (gidx + 1) * LANES])
        kf_o[:, sl] = k
        kb_o[:, sl] = k.astype(BF16)
        v = p[:, 3 * a + gidx * LANES:3 * a + (gidx + 1) * LANES]
        vf_o[:, sl] = v
        vb_o[:, sl] = v.astype(BF16)
    ik = rope(p[:, 4 * a:4 * a + LANES])
    ikf_o[...] = ik[:, :IDX_DIM]
    ik2_o[...] = ik.astype(BF16)
    iw_o[...] = p[:, 4 * a + LANES:4 * a + 2 * LANES] * IDX_W_SCALE


def _mixer_proj(x, g, w2, angles, tm=512):
    t, d = x.shape
    ncol = w2.shape[1]
    row = lambda width: pl.BlockSpec((tm, width), lambda i: (i, 0))
    a = ATTN_DIM
    outs = [
        jax.ShapeDtypeStruct((t, a), BF16),
        jax.ShapeDtypeStruct((t, a), BF16),
        jax.ShapeDtypeStruct((t, a), F32),
        jax.ShapeDtypeStruct((t, a), BF16),
        jax.ShapeDtypeStruct((t, a), F32),
        jax.ShapeDtypeStruct((t, a), BF16),
        jax.ShapeDtypeStruct((t, IDX_DIM), F32),
        jax.ShapeDtypeStruct((t, LANES), BF16),
        jax.ShapeDtypeStruct((t, LANES), F32),
    ]
    return pl.pallas_call(
        _proj_kernel,
        out_shape=outs,
        grid=(t // tm,),
        in_specs=[row(d), _const_spec((1, d)), _const_spec((d, ncol)), row(LANES)],
        out_specs=[row(a), row(a), row(a), row(a), row(a), row(a), row(IDX_DIM), row(LANES), row(LANES)],
        compiler_params=pltpu.CompilerParams(dimension_semantics=("parallel",),
                                             vmem_limit_bytes=VMEM_LIMIT),
        name="mixer_proj",
    )(x, g, w2, angles)


def _sortable(x):
    b = lax.bitcast_convert_type(x, I32)
    k = b ^ ((b >> 31) & 0x7FFFFFFF)
    return jnp.where(x == 0.0, 0, k)


def _head_lhs(x, h):
    grp, odd = divmod(h, 2)
    xg = x[:, grp * LANES:(grp + 1) * LANES].astype(F32)
    lane = lax.broadcasted_iota(I32, xg.shape, 1)
    keep = (lane >= HEAD_DIM) if odd else (lane < HEAD_DIM)
    return jnp.where(keep, xg, 0.0).astype(BF16)


def _stack_heads(x_ref, dst_ref, rows):
    x = x_ref[...]
    for h in range(N_HEADS):
        dst_ref[h * rows:(h + 1) * rows, :] = _head_lhs(x, h)


def _spread_weights(w, wrep_ref):
    for h in range(IDX_HEADS):
        wrep_ref[h] = jnp.broadcast_to(w[:, h:h + 1], wrep_ref.shape[1:])


def _index_dots(iqs_ref, ikc):
    return lax.dot_general(iqs_ref[...], ikc, _NT, preferred_element_type=F32)


def _index_combine(d, wrep_ref):
    rows = wrep_ref.shape[1]
    groups = []
    for l in range(d.shape[1] // LANES):
        acc = None
        for h in range(IDX_HEADS):
            term = jnp.maximum(d[h * rows:(h + 1) * rows, l * LANES:(l + 1) * LANES], 0.0) * wrep_ref[h]
            acc = term if acc is None else acc + term
        groups.append(acc)
    return jnp.concatenate(groups, axis=1)


def _index_scores(iqs_ref, wrep_ref, ikc):
    return _index_combine(_index_dots(iqs_ref, ikc), wrep_ref)


def _lane_fold(x):
    out = x[:, :LANES]
    for l in range(1, x.shape[1] // LANES):
        out = out + x[:, l * LANES:(l + 1) * LANES]
    return out


COUNT_ROWS = 32


def _count_tile(tile_ref, cand):
    rows = tile_ref.shape[0]
    step = min(COUNT_ROWS, rows)
    parts = []
    for r in range(0, rows, step):
        hit = jnp.where(tile_ref[r:r + step, :] >= cand[r:r + step], 1.0, 0.0)
        parts.append(_lane_fold(hit))
    return jnp.concatenate(parts, axis=0)


def _select_threshold(count_ge, n_adm, topk):
    c0 = count_ge(jnp.zeros(n_adm.shape, I32))
    c1 = count_ge(jnp.ones(n_adm.shape, I32))
    nonneg = c0 >= topk
    zero_tie = jnp.logical_and(nonneg, c1 < topk)

    def cond(st):
        i, _, cnt = st
        return jnp.logical_and(i < 32, jnp.max(jnp.where(zero_tie, 0.0, cnt)) > topk)

    def one_bit(i, t, cnt):
        cand = t + jnp.left_shift(jnp.int32(1), 31 - i)
        c = count_ge(cand)
        ok = c >= topk
        return jnp.where(ok, cand, t), jnp.where(ok, c, cnt)

    def step(st):
        i, t, cnt = st
        t, cnt = one_bit(i, t, cnt)
        t, cnt = one_bit(i + 1, t, cnt)
        return i + 2, t, cnt

    t, cnt = one_bit(1, jnp.where(nonneg, 0, INT_MIN), jnp.where(nonneg, c0, n_adm))
    _, t, cnt = lax.while_loop(cond, step, (jnp.int32(2), t, cnt))
    return jnp.maximum(t, INT_MIN + 1), cnt


def _bias_plain(key, t):
    return jnp.where(key >= t, 0.0, NEG_BIG)


def _init_tri(tri_ref):
    @pl.when(pl.program_id(0) == 0)
    def _():
        r = lax.broadcasted_iota(I32, tri_ref.shape, 0)
        c = lax.broadcasted_iota(I32, tri_ref.shape, 1)
        tri_ref[...] = jnp.where(jnp.logical_or(r < c, c >= LANES), 1.0, 0.0).astype(BF16)


def _tie_counts(key, t, tri_ref):
    eq = key == t
    eqb = jnp.where(eq, 1.0, 0.0).astype(BF16)
    stacked = jnp.concatenate([eqb[:, l * LANES:(l + 1) * LANES] for l in range(key.shape[1] // LANES)], axis=0)
    return eq, jnp.dot(stacked, tri_ref[...], preferred_element_type=F32)


def _tie_bias(key, t, eq, counts, need, run):
    rows, sc = key.shape
    before = []
    for l in range(sc // LANES):
        blk = counts[l * rows:(l + 1) * rows]
        before.append(run + blk[:, :LANES])
        run = run + blk[:, LANES:]
    sel = (key > t) | (eq & (jnp.concatenate(before, axis=1) < need))
    return jnp.where(sel, 0.0, NEG_BIG), run


def _bias_ties(key, t, need, run, tri_ref):
    eq, counts = _tie_counts(key, t, tri_ref)
    return _tie_bias(key, t, eq, counts, need, run)


def _pair_scores(qs_ref, kc):
    pairs = N_HEADS // 2
    prow = qs_ref.shape[0] // pairs
    return [lax.dot_general(qs_ref[g * prow:(g + 1) * prow, :], kc[:, g * LANES:(g + 1) * LANES], _NT,
                            preferred_element_type=F32) for g in range(pairs)]


def _softmax_pv(units, m_ref, acc_ref):
    probs, alphas = [], []
    for u, (s, bias, _) in enumerate(units):
        s = s + bias
        m_prev = m_ref[u]
        m_next = jnp.maximum(m_prev, jnp.max(s, axis=1, keepdims=True))
        alphas.append(jnp.exp2(m_prev - m_next))
        probs.append(jnp.exp2(s - jnp.concatenate([m_next] * (s.shape[1] // LANES), axis=1)).astype(BF16))
        m_ref[u] = m_next
    for u, (_, _, vext) in enumerate(units):
        pv = jnp.dot(probs[u], vext, preferred_element_type=F32)
        acc_ref[u] = jnp.concatenate([alphas[u]] * (vext.shape[1] // LANES), axis=1) * acc_ref[u] + pv


def _pair_units(scores, bias, vc):
    bias2 = jnp.concatenate([bias, bias], axis=0)
    ones = jnp.ones((vc.shape[0], LANES), BF16)
    return [(s, bias2, jnp.concatenate([vc[:, g * LANES:(g + 1) * LANES], ones], axis=1))
            for g, s in enumerate(scores)]


def _init_attend(m_ref, acc_ref):
    m_ref[...] = jnp.full(m_ref.shape, NEG_BIG, F32)
    acc_ref[...] = jnp.zeros(acc_ref.shape, F32)


def _finish_attend(o_ref, acc_ref):
    rows = acc_ref.shape[1] // 2
    lane = lax.broadcasted_iota(I32, (rows, LANES), 1)
    for g in range(N_HEADS // 2):
        acc = acc_ref[g]
        o = acc[:, :LANES] / acc[:, LANES:]
        o_ref[:, g * LANES:(g + 1) * LANES] = jnp.where(lane < HEAD_DIM, o[:rows], o[rows:]).astype(o_ref.dtype)


def _prompt_attn_kernel(q_ref, iq_ref, iw_ref, kb_ref, vb_ref, ik2_ref, o_ref,
                        key_ref, iqs_ref, qs_ref, wrep_ref, m_ref, acc_ref, tri_ref, *, tq, sc, topk):
    j = pl.program_id(0)
    n_chunks = ((j + 1) * tq + sc - 1) // sc

    _init_tri(tri_ref)
    _stack_heads(iq_ref, iqs_ref, tq)
    _stack_heads(q_ref, qs_ref, tq)
    _spread_weights(iw_ref[...], wrep_ref)
    row = lax.broadcasted_iota(I32, (tq, 1), 0)
    limit = j * tq + (row // CHUNK + 1) * CHUNK

    def pairs_then_tail(fn, carry=0):
        carry = lax.fori_loop(0, n_chunks // 2, lambda i, carry: fn((2 * i, 2 * i + 1), carry), carry)

        @pl.when(n_chunks % 2 == 1)
        def _():
            fn((n_chunks - 1,), carry)

    def score_chunks(chunks, carry):
        offs = [pl.multiple_of(c * sc, sc) for c in chunks]
        dots = [_index_dots(iqs_ref, ik2_ref[pl.ds(off, sc), :]) for off in offs]
        for c, off, d in zip(chunks, offs, dots):
            sidx = off + lax.broadcasted_iota(I32, (tq, sc), 1)
            key_ref[c] = jnp.where(sidx < limit, _sortable(_index_combine(d, wrep_ref)), INT_MIN)
        return carry
    pairs_then_tail(score_chunks)

    def count_ge(cand):
        def body(c, acc):
            return acc + _count_tile(key_ref.at[c], cand)
        part = lax.fori_loop(0, n_chunks, body, jnp.zeros((tq, LANES), F32))
        return jnp.sum(part, axis=1, keepdims=True)

    t, n_ge = _select_threshold(count_ge, limit.astype(F32), topk)
    has_ties = jnp.max(n_ge) > topk

    @pl.when(jnp.logical_not(has_ties))
    def _():
        def body(c, carry):
            key_ref[c] = lax.bitcast_convert_type(_bias_plain(key_ref[c], t), I32)
            return carry
        lax.fori_loop(0, n_chunks, body, 0)

    @pl.when(has_ties)
    def _():
        need = topk - count_ge(t + 1)

        def tie_chunks(chunks, run):
            keys = [key_ref[c] for c in chunks]
            counted = [_tie_counts(key, t, tri_ref) for key in keys]
            for c, key, (eq, counts) in zip(chunks, keys, counted):
                bias, run = _tie_bias(key, t, eq, counts, need, run)
                key_ref[c] = lax.bitcast_convert_type(bias, I32)
            return run
        pairs_then_tail(tie_chunks, jnp.zeros((tq, LANES), F32))

    _init_attend(m_ref, acc_ref)

    def attend_chunks(chunks, carry):
        offs = [pl.multiple_of(c * sc, sc) for c in chunks]
        scores = [_pair_scores(qs_ref, kb_ref[pl.ds(off, sc), :]) for off in offs]
        for c, off, s in zip(chunks, offs, scores):
            bias = lax.bitcast_convert_type(key_ref[c], F32)
            _softmax_pv(_pair_units(s, bias, vb_ref[pl.ds(off, sc), :]), m_ref, acc_ref)
        return carry
    pairs_then_tail(attend_chunks)
    _finish_attend(o_ref, acc_ref)


def _attn_scratch(rows):
    pairs = N_HEADS // 2
    return [
        pltpu.VMEM((N_HEADS * rows, LANES), BF16),
        pltpu.VMEM((N_HEADS * rows, LANES), BF16),
        pltpu.VMEM((IDX_HEADS, rows, LANES), F32),
        pltpu.VMEM((pairs, 2 * rows, LANES), F32),
        pltpu.VMEM((pairs, 2 * rows, 2 * LANES), F32),
        pltpu.VMEM((LANES, 2 * LANES), BF16),
    ]


def _prompt_attention(q, iq, iw, kb, vb, ik2, topk, tq=128, sc=512):
    s = kb.shape[0]
    row = lambda width: pl.BlockSpec((tq, width), lambda j: (j, 0))
    whole = pl.BlockSpec(memory_space=pltpu.VMEM)
    return pl.pallas_call(
        functools.partial(_prompt_attn_kernel, tq=tq, sc=sc, topk=topk),
        out_shape=jax.ShapeDtypeStruct((s, ATTN_DIM), BF16),
        grid=(s // tq,),
        in_specs=[row(ATTN_DIM), row(ATTN_DIM), row(LANES), whole, whole, whole],
        out_specs=row(ATTN_DIM),
        scratch_shapes=[pltpu.VMEM((s // sc, tq, sc), I32)] + _attn_scratch(tq),
        compiler_params=pltpu.CompilerParams(dimension_semantics=("arbitrary",),
                                             vmem_limit_bytes=VMEM_LIMIT),
        name="prompt_attention",
    )(q, iq, iw, kb, vb, ik2)


def _sample_attn_kernel(q_ref, iq_ref, iw_ref, kn_ref, vn_ref, ikn_ref, ck_ref, cv_ref, cik_ref, o_ref,
                        key_ref, keyn_ref, iqs_ref, qh_ref, wrep_ref, m_ref, acc_ref, tri_ref, *, sc, topk):
    ts = q_ref.shape[0]
    past = cik_ref.shape[1]
    n_chunks = past // sc

    _init_tri(tri_ref)
    _stack_heads(iq_ref, iqs_ref, ts)
    _spread_weights(iw_ref[...], wrep_ref)

    pad = lambda x: jnp.concatenate([x, jnp.zeros((LANES - ts, x.shape[1]), x.dtype)], axis=0)
    new_ok = lax.broadcasted_iota(I32, (ts, LANES), 1) < ts

    def cache_ik(c):
        ik = cik_ref[0, c * sc:(c + 1) * sc, :].astype(BF16)
        return jnp.concatenate([ik, ik], axis=1)

    for c in range(n_chunks):
        key_ref[c] = _sortable(_index_scores(iqs_ref, wrep_ref, cache_ik(c)))
    keyn_ref[...] = jnp.where(new_ok, _sortable(_index_scores(iqs_ref, wrep_ref, pad(ikn_ref[...]))), INT_MIN)

    def count_ge(cand):
        part = jnp.where(keyn_ref[...] >= cand, 1.0, 0.0)
        for c in range(n_chunks):
            part = part + _count_tile(key_ref.at[c], cand)
        return jnp.sum(part, axis=1, keepdims=True)

    t, n_ge = _select_threshold(count_ge, jnp.full((ts, 1), past + ts, F32), topk)
    has_ties = jnp.max(n_ge) > topk

    @pl.when(jnp.logical_not(has_ties))
    def _():
        for c in range(n_chunks):
            key_ref[c] = lax.bitcast_convert_type(_bias_plain(key_ref[c], t), I32)
        keyn_ref[...] = lax.bitcast_convert_type(_bias_plain(keyn_ref[...], t), I32)

    @pl.when(has_ties)
    def _():
        need = topk - count_ge(t + 1)
        run = jnp.zeros((ts, LANES), F32)
        for c in range(n_chunks):
            bias, run = _bias_ties(key_ref[c], t, need, run, tri_ref)
            key_ref[c] = lax.bitcast_convert_type(bias, I32)
        bias, run = _bias_ties(keyn_ref[...], t, need, run, tri_ref)
        keyn_ref[...] = lax.bitcast_convert_type(bias, I32)

    head = lambda x, h: x[:, h * HEAD_DIM:(h + 1) * HEAD_DIM]
    q = q_ref[...]
    for h in range(N_HEADS):
        qh_ref[h] = head(q, h)

    def head_units(bias, k_of, v_of):
        scores = [lax.dot_general(qh_ref[h], k_of(h), _NT, preferred_element_type=F32) for h in range(N_HEADS)]
        ones = jnp.ones((bias.shape[1], HEAD_DIM), BF16)
        return [(scores[h], bias, jnp.concatenate([v_of(h), ones], axis=1)) for h in range(N_HEADS)]

    _init_attend(m_ref, acc_ref)
    for c in range(n_chunks):
        rows = lambda ref, h: ref[0, pl.ds(c * sc * N_HEADS + h, sc, stride=N_HEADS), :].astype(BF16)
        _softmax_pv(head_units(lax.bitcast_convert_type(key_ref[c], F32),
                               functools.partial(rows, ck_ref), functools.partial(rows, cv_ref)), m_ref, acc_ref)
    kn, vn = kn_ref[...], vn_ref[...]
    _softmax_pv(head_units(lax.bitcast_convert_type(keyn_ref[...], F32),
                           lambda h: pad(head(kn, h)), lambda h: pad(head(vn, h))), m_ref, acc_ref)
    for h in range(N_HEADS):
        acc = acc_ref[h]
        o_ref[:, h * HEAD_DIM:(h + 1) * HEAD_DIM] = (acc[:, :HEAD_DIM] / acc[:, HEAD_DIM:]).astype(o_ref.dtype)


def _sample_attention(q, iq, iw, kb, vb, ik2, cache_k, cache_v, cache_ik, ts, topk, sc=512):
    t = q.shape[0]
    nb, past = cache_ik.shape[:2]
    row = lambda width: pl.BlockSpec((ts, width), lambda b: (b, 0))
    cache = lambda rows, width: pl.BlockSpec((1, rows, width), lambda b: (b, 0, 0))
    return pl.pallas_call(
        functools.partial(_sample_attn_kernel, sc=sc, topk=topk),
        out_shape=jax.ShapeDtypeStruct((t, ATTN_DIM), BF16),
        grid=(nb,),
        in_specs=[row(ATTN_DIM), row(ATTN_DIM), row(LANES), row(ATTN_DIM), row(ATTN_DIM), row(LANES),
                  cache(past * N_HEADS, HEAD_DIM), cache(past * N_HEADS, HEAD_DIM), cache(past, IDX_DIM)],
        out_specs=row(ATTN_DIM),
        scratch_shapes=[
            pltpu.VMEM((past // sc, ts, sc), I32),
            pltpu.VMEM((ts, LANES), I32),
            pltpu.VMEM((N_HEADS * ts, LANES), BF16),
            pltpu.VMEM((N_HEADS, ts, HEAD_DIM), BF16),
            pltpu.VMEM((IDX_HEADS, ts, LANES), F32),
            pltpu.VMEM((N_HEADS, ts, LANES), F32),
            pltpu.VMEM((N_HEADS, ts, LANES), F32),
            pltpu.VMEM((LANES, 2 * LANES), BF16),
        ],
        compiler_params=pltpu.CompilerParams(dimension_semantics=("arbitrary",),
                                             vmem_limit_bytes=VMEM_LIMIT),
        name="sample_attention",
    )(q, iq, iw, kb, vb, ik2, cache_k, cache_v, cache_ik)


HIST_ROW0 = 2


def _merge_kernel(x_ref, a_ref, hist_ref, g_ref, wglu_ref, wgate_ref, bgate_ref, wa_ref,
                  dww_ref, dwb_ref, lng_ref, lnb_ref, wc_ref, wo_ref, postg_ref,
                  y_ref, conv_o_ref, uext_ref, *, nseq, seqlen, carry):
    i = pl.program_id(0)
    x = x_ref[...]
    n = _rms(x, g_ref[...]).astype(BF16)
    glu = jnp.dot(n, wglu_ref[...], preferred_element_type=F32)
    u = glu[:, :CONV_CH] * _sigmoid(glu[:, CONV_CH:])

    new0 = HIST_ROW0 + HIST
    if carry:
        @pl.when(i == 0)
        def _():
            uext_ref[:, HIST_ROW0:new0, :] = hist_ref[...]

        @pl.when(i > 0)
        def _():
            uext_ref[:, HIST_ROW0:new0, :] = uext_ref[:, seqlen + HIST_ROW0:seqlen + new0, :]
    else:
        uext_ref[:, HIST_ROW0:new0, :] = hist_ref[...]
    uext_ref[:, new0:new0 + seqlen, :] = u.reshape(nseq, seqlen, CONV_CH)
    conv_o_ref[...] = uext_ref[:, seqlen + HIST_ROW0:seqlen + new0, :]

    dww = dww_ref[...]
    cdw = None
    for tap in range(CONV_WIDTH):
        term = uext_ref[:, HIST_ROW0 + tap:HIST_ROW0 + tap + seqlen, :] * dww[tap:tap + 1, :]
        cdw = term if cdw is None else cdw + term
    cdw = cdw.reshape(nseq * seqlen, CONV_CH) + dwb_ref[...]

    mu = jnp.mean(cdw, axis=-1, keepdims=True)
    xc = cdw - mu
    var = jnp.mean(xc * xc, axis=-1, keepdims=True)
    ln = xc * lax.rsqrt(var + EPS) * lng_ref[...] + lnb_ref[...]
    cact = (ln * _sigmoid(ln)).astype(BF16)
    c_proj = jnp.dot(cact, wc_ref[...], preferred_element_type=F32)
    a_proj = jnp.dot(a_ref[...], wa_ref[...], preferred_element_type=F32)

    d = x.shape[1]
    gate = _sigmoid(jnp.dot(n, wgate_ref[...], preferred_element_type=F32) + bgate_ref[...])
    merged = (gate[:, :d] * a_proj + gate[:, d:] * c_proj).astype(BF16)
    y = jnp.dot(merged, wo_ref[...], preferred_element_type=F32)
    y_ref[...] = x + _rms(y, postg_ref[...])


def _branch_merge(x, a, hist, g, wglu, wgate, bgate, wa, dww, dwb, lng, lnb, wc, wo, postg,
                  *, nseq, seqlen, carry):
    t, d = x.shape
    tm = nseq * seqlen
    row = lambda width: pl.BlockSpec((tm, width), lambda i: (i, 0))
    if carry:
        seq_spec = pl.BlockSpec((nseq, HIST, CONV_CH), lambda i: (0, 0, 0))
        n_state = nseq
    else:
        seq_spec = pl.BlockSpec((nseq, HIST, CONV_CH), lambda i: (i, 0, 0))
        n_state = t // seqlen
    consts = [g, wglu, wgate, bgate, wa, dww, dwb, lng, lnb, wc, wo, postg]
    return pl.pallas_call(
        functools.partial(_merge_kernel, nseq=nseq, seqlen=seqlen, carry=carry),
        out_shape=[jax.ShapeDtypeStruct((t, d), F32),
                   jax.ShapeDtypeStruct((n_state, HIST, CONV_CH), F32)],
        grid=(t // tm,),
        in_specs=[row(d), row(ATTN_DIM), seq_spec] + [_const_spec(c.shape) for c in consts],
        out_specs=[row(d), seq_spec],
        scratch_shapes=[pltpu.VMEM((nseq, HIST_ROW0 + HIST + seqlen, CONV_CH), F32)],
        compiler_params=pltpu.CompilerParams(dimension_semantics=("arbitrary",),
                                             vmem_limit_bytes=VMEM_LIMIT),
        name="branch_merge",
    )(x, a, hist, *consts)


def kernel(x_prompt, x_sample, cache_k, cache_v, cache_idx_k, state_conv,
           ffn1_pre_g, ffn1_w_gate, ffn1_w_up, ffn1_w_down, ffn1_post_g,
           mix_pre_g, w_in, b_gate, w_attn_o, conv_dw_w, conv_dw_b, conv_ln_g, conv_ln_b,
           w_conv_o, w_out, mix_post_g,
           ffn2_pre_g, ffn2_w_gate, ffn2_w_up, ffn2_w_down, ffn2_post_g):
    bp, lp, d = x_prompt.shape
    db, ts, _ = x_sample.shape
    depth, _, past = cache_k.shape[:3]
    assert bp == 1 and depth == 1
    topk_p = min(TOPK_MAX, lp // 4)
    topk_s = min(TOPK_MAX, (past + ts) // 4)

    vec = lambda a: a.reshape(1, -1)
    bf = lambda a: a.astype(BF16)
    ffn1 = (vec(ffn1_pre_g[0]), bf(ffn1_w_gate[0]), bf(ffn1_w_up[0]), bf(ffn1_w_down[0]), vec(ffn1_post_g[0]))
    ffn2 = (vec(ffn2_pre_g[0]), bf(ffn2_w_gate[0]), bf(ffn2_w_up[0]), bf(ffn2_w_down[0]), vec(ffn2_post_g[0]))

    widths = (IDX_HEADS * IDX_DIM, IDX_HEADS, IDX_DIM, ATTN_DIM, ATTN_DIM, ATTN_DIM, 2 * CONV_CH, 2 * d)
    offs = [0]
    for wd_ in widths:
        offs.append(offs[-1] + wd_)
    wi = w_in[0]
    col = lambda k: wi[:, offs[k]:offs[k + 1]]
    w_iw = jnp.pad(col(1), ((0, 0), (0, LANES - IDX_HEADS)))
    w2 = bf(jnp.concatenate([col(0), col(3), col(4), col(5), col(2), col(2), w_iw], axis=1))
    merge_w = (vec(mix_pre_g[0]), bf(col(6)), bf(col(7)), vec(b_gate[0]), bf(w_attn_o[0]),
               conv_dw_w[0], vec(conv_dw_b[0]), vec(conv_ln_g[0]), vec(conv_ln_b[0]),
               bf(w_conv_o[0]), bf(w_out[0]), vec(mix_post_g[0]))

    pos_p = jnp.arange(lp, dtype=I32)
    pos_s = jnp.tile(past + jnp.arange(ts, dtype=I32), db)

    hp = _ffn_half(x_prompt.reshape(lp, d), *ffn1)
    iq, q, kf, kb, vf, vb, ikf, ik2, iw = _mixer_proj(hp, vec(mix_pre_g[0]), w2, _rope_angles(pos_p))
    a = _prompt_attention(q, iq, iw, kb, vb, ik2, topk_p)
    hp, conv_p = _branch_merge(hp, a, jnp.zeros((1, HIST, CONV_CH), F32), *merge_w,
                               nseq=1, seqlen=512, carry=True)
    hp = _ffn_half(hp, *ffn2)
    outs_p = (kf.reshape(1, 1, lp, N_HEADS, HEAD_DIM), vf.reshape(1, 1, lp, N_HEADS, HEAD_DIM),
              ikf.reshape(1, 1, lp, IDX_DIM), conv_p.reshape(1, 1, HIST, CONV_CH))

    n_s = db * ts
    hs = _ffn_half(x_sample.reshape(n_s, d), *ffn1)
    iq, q, kf, kb, vf, vb, ikf, ik2, iw = _mixer_proj(hs, vec(mix_pre_g[0]), w2, _rope_angles(pos_s))
    a = _sample_attention(q, iq, iw, kb, vb, ik2, cache_k[0].reshape(db, past * N_HEADS, HEAD_DIM),
                          cache_v[0].reshape(db, past * N_HEADS, HEAD_DIM), cache_idx_k[0], ts, topk_s)
    hs, conv_s = _branch_merge(hs, a, state_conv[0], *merge_w, nseq=8, seqlen=ts, carry=False)
    hs = _ffn_half(hs, *ffn2)
    outs_s = (kf.reshape(1, db, ts, N_HEADS, HEAD_DIM), vf.reshape(1, db, ts, N_HEADS, HEAD_DIM),
              ikf.reshape(1, db, ts, IDX_DIM), conv_s.reshape(1, db, HIST, CONV_CH))

    return (hp.reshape(1, lp, d), hs.reshape(db, ts, d)) + outs_p + outs_s
```

```python
import functools
import math

import jax
import jax.numpy as jnp
from jax import lax
from jax.experimental import pallas as pl
from jax.experimental.pallas import tpu as pltpu

F32 = jnp.float32
BF16 = jnp.bfloat16
I32 = jnp.int32

CHUNK = 64
N_HEADS = 8
HEAD_DIM = 64
ATTN_DIM = N_HEADS * HEAD_DIM
IDX_HEADS = 8
IDX_DIM = 64
IDX_W_SCALE = (IDX_HEADS * IDX_DIM) ** -0.5
TOPK_MAX = 256
ROPE_THETA = 500000.0
ROT = HEAD_DIM // 4
CONV_CH = 512
CONV_WIDTH = 31
HIST = CONV_WIDTH - 1
EPS = 1e-6

LANES = 128
VMEM_LIMIT = 56 * 1024 * 1024
INT_MIN = -(2 ** 31)
NEG_BIG = -1e30
Q_SCALE = HEAD_DIM ** -0.5 * math.log2(math.e)

_NT = (((1,), (1,)), ((), ()))


def _rms(x, g):
    return x * lax.rsqrt(jnp.mean(x * x, axis=-1, keepdims=True) + EPS) * g


def _sigmoid(x):
    return 1.0 / (1.0 + jnp.exp(-x))


def _const_spec(shape):
    nd = len(shape)
    return pl.BlockSpec(shape, lambda *_: (0,) * nd, pipeline_mode=pl.Buffered(1))


def _ffn_kernel(x_ref, pre_g_ref, wg_ref, wu_ref, wd_ref, post_g_ref, o_ref):
    x = x_ref[...]
    h = _rms(x, pre_g_ref[...]).astype(BF16)
    g = jnp.dot(h, wg_ref[...], preferred_element_type=F32)
    u = jnp.dot(h, wu_ref[...], preferred_element_type=F32)
    a = (g * _sigmoid(g) * u).astype(BF16)
    f = jnp.dot(a, wd_ref[...], preferred_element_type=F32)
    o_ref[...] = x + 0.5 * _rms(f, post_g_ref[...])


def _ffn_half(x, pre_g, wg, wu, wd, post_g, tm=512):
    t, d = x.shape
    dff = wg.shape[1]
    row = pl.BlockSpec((tm, d), lambda i: (i, 0))
    return pl.pallas_call(
        _ffn_kernel,
        out_shape=jax.ShapeDtypeStruct((t, d), F32),
        grid=(t // tm,),
        in_specs=[row, _const_spec((1, d)), _const_spec((d, dff)), _const_spec((d, dff)),
                  _const_spec((dff, d)), _const_spec((1, d))],
        out_specs=row,
        compiler_params=pltpu.CompilerParams(dimension_semantics=("parallel",),
                                             vmem_limit_bytes=VMEM_LIMIT),
        name="ffn_half",
    )(x, pre_g, wg, wu, wd, post_g)


def _rope_angles(pos):
    half = ROT // 2
    inv_freq = jnp.exp(-math.log(ROPE_THETA) * jnp.arange(half, dtype=F32) * (2.0 / ROT))
    ang = pos.astype(F32)[:, None] * inv_freq[None, :]
    return jnp.pad(jnp.concatenate([jnp.cos(ang), jnp.sin(ang)], axis=1), ((0, 0), (0, LANES - ROT)))


def _rope_tables(cs):
    half = ROT // 2
    lane = lax.broadcasted_iota(I32, cs.shape, 1)
    both = lambda a: a + pltpu.roll(a, HEAD_DIM, 1)
    cos_lo = jnp.where(lane < half, cs, 0.0)
    sin_hi = jnp.where(jnp.logical_and(lane >= half, lane < ROT), cs, 0.0)
    c = both(cos_lo + pltpu.roll(cos_lo, half, 1))
    c = jnp.where(lane % HEAD_DIM < ROT, c, 1.0)
    s_dn = both(sin_hi)
    s_up = -both(pltpu.roll(sin_hi, LANES - half, 1))
    return c, s_up, s_dn


def _proj_kernel(x_ref, g_ref, w_ref, cs_ref,
                 iq_o, q_o, kf_o, kb_o, vf_o, vb_o, ikf_o, ik2_o, iw_o):
    n = _rms(x_ref[...], g_ref[...]).astype(BF16)
    p = jnp.dot(n, w_ref[...], preferred_element_type=F32)
    c, s_up, s_dn = _rope_tables(cs_ref[...])

    def rope(x):
        return x * c + pltpu.roll(x, LANES - ROT // 2, 1) * s_up + pltpu.roll(x, ROT // 2, 1) * s_dn

    a = ATTN_DIM
    for gidx in range(a // LANES):
        sl = slice(gidx * LANES, (gidx + 1) * LANES)
        iq_o[:, sl] = rope(p[:, sl]).astype(BF16)
        q_o[:, sl] = (rope(p[:, a + gidx * LANES:a + (gidx + 1) * LANES]) * Q_SCALE).astype(BF16)
        k = rope(p[:, 2 * a + gidx * LANES:2 * a + (gidx + 1) * LANES])
        kf_o[:, sl] = k
        kb_o[:, sl] = k.astype(BF16)
        v = p[:, 3 * a + gidx * LANES:3 * a + (gidx + 1) * LANES]
        vf_o[:, sl] = v
        vb_o[:, sl] = v.astype(BF16)
    ik = rope(p[:, 4 * a:4 * a + LANES])
    ikf_o[...] = ik[:, :IDX_DIM]
    ik2_o[...] = ik.astype(BF16)
    iw_o[...] = p[:, 4 * a + LANES:4 * a + 2 * LANES] * IDX_W_SCALE


def _mixer_proj(x, g, w2, angles, tm=512):
    t, d = x.shape
    ncol = w2.shape[1]
    row = lambda width: pl.BlockSpec((tm, width), lambda i: (i, 0))
    a = ATTN_DIM
    outs = [
        jax.ShapeDtypeStruct((t, a), BF16),
        jax.ShapeDtypeStruct((t, a), BF16),
        jax.ShapeDtypeStruct((t, a), F32),
        jax.ShapeDtypeStruct((t, a), BF16),
        jax.ShapeDtypeStruct((t, a), F32),
        jax.ShapeDtypeStruct((t, a), BF16),
        jax.ShapeDtypeStruct((t, IDX_DIM), F32),
        jax.ShapeDtypeStruct((t, LANES), BF16),
        jax.ShapeDtypeStruct((t, LANES), F32),
    ]
    return pl.pallas_call(
        _proj_kernel,
        out_shape=outs,
        grid=(t // tm,),
        in_specs=[row(d), _const_spec((1, d)), _const_spec((d, ncol)), row(LANES)],
        out_specs=[row(a), row(a), row(a), row(a), row(a), row(a), row(IDX_DIM), row(LANES), row(LANES)],
        compiler_params=pltpu.CompilerParams(dimension_semantics=("parallel",),
                                             vmem_limit_bytes=VMEM_LIMIT),
        name="mixer_proj",
    )(x, g, w2, angles)


def _sortable(x):
    b = lax.bitcast_convert_type(x, I32)
    k = b ^ ((b >> 31) & 0x7FFFFFFF)
    return jnp.where(x == 0.0, 0, k)


def _head_lhs(x, h):
    grp, odd = divmod(h, 2)
    xg = x[:, grp * LANES:(grp + 1) * LANES].astype(F32)
    lane = lax.broadcasted_iota(I32, xg.shape, 1)
    keep = (lane >= HEAD_DIM) if odd else (lane < HEAD_DIM)
    return jnp.where(keep, xg, 0.0).astype(BF16)


def _stack_heads(x_ref, dst_ref, rows):
    x = x_ref[...]
    for h in range(N_HEADS):
        dst_ref[h * rows:(h + 1) * rows, :] = _head_lhs(x, h)


def _spread_weights(w, wrep_ref):
    for h in range(IDX_HEADS):
        wrep_ref[h] = jnp.broadcast_to(w[:, h:h + 1], wrep_ref.shape[1:])


def _index_dots(iqs_ref, ikc):
    return lax.dot_general(iqs_ref[...], ikc, _NT, preferred_element_type=F32)


def _index_combine(d, wrep_ref):
    rows = wrep_ref.shape[1]
    groups = []
    for l in range(d.shape[1] // LANES):
        acc = None
        for h in range(IDX_HEADS):
            term = jnp.maximum(d[h * rows:(h + 1) * rows, l * LANES:(l + 1) * LANES], 0.0) * wrep_ref[h]
            acc = term if acc is None else acc + term
        groups.append(acc)
    return jnp.concatenate(groups, axis=1)


def _index_scores(iqs_ref, wrep_ref, ikc):
    return _index_combine(_index_dots(iqs_ref, ikc), wrep_ref)


def _lane_fold(x):
    out = x[:, :LANES]
    for l in range(1, x.shape[1] // LANES):
        out = out + x[:, l * LANES:(l + 1) * LANES]
    return out


COUNT_ROWS = 32


def _count_tile(tile_ref, cand):
    rows = tile_ref.shape[0]
    step = min(COUNT_ROWS, rows)
    parts = []
    for r in range(0, rows, step):
        hit = jnp.where(tile_ref[r:r + step, :] >= cand[r:r + step], 1.0, 0.0)
        parts.append(_lane_fold(hit))
    return jnp.concatenate(parts, axis=0)


def _count_tile_01(tile_ref):
    rows = tile_ref.shape[0]
    step = min(COUNT_ROWS, rows)
    ge0, ge1 = [], []
    for r in range(0, rows, step):
        tile = tile_ref[r:r + step, :]
        ge0.append(_lane_fold(jnp.where(tile >= 0, 1.0, 0.0)))
        ge1.append(_lane_fold(jnp.where(tile >= 1, 1.0, 0.0)))
    return jnp.concatenate(ge0, axis=0), jnp.concatenate(ge1, axis=0)


def _select_threshold(count_ge, count_01, n_adm, topk):
    c0, c1 = count_01()
    nonneg = c0 >= topk
    zero_tie = jnp.logical_and(nonneg, c1 < topk)

    def cond(st):
        i, _, cnt = st
        return jnp.logical_and(i < 32, jnp.max(jnp.where(zero_tie, 0.0, cnt)) > topk)

    def one_bit(i, t, cnt):
        cand = t + jnp.left_shift(jnp.int32(1), 31 - i)
        c = count_ge(cand)
        ok = c >= topk
        return jnp.where(ok, cand, t), jnp.where(ok, c, cnt)

    def step(st):
        i, t, cnt = st
        t, cnt = one_bit(i, t, cnt)
        t, cnt = one_bit(i + 1, t, cnt)
        return i + 2, t, cnt

    t, cnt = one_bit(1, jnp.where(nonneg, 0, INT_MIN), jnp.where(nonneg, c0, n_adm))
    _, t, cnt = lax.while_loop(cond, step, (jnp.int32(2), t, cnt))
    return jnp.maximum(t, INT_MIN + 1), cnt


def _bias_plain(key, t):
    return jnp.where(key >= t, 0.0, NEG_BIG)


def _init_tri(tri_ref):
    @pl.when(pl.program_id(0) == 0)
    def _():
        r = lax.broadcasted_iota(I32, tri_ref.shape, 0)
        c = lax.broadcasted_iota(I32, tri_ref.shape, 1)
        tri_ref[...] = jnp.where(jnp.logical_or(r < c, c >= LANES), 1.0, 0.0).astype(BF16)


def _tie_counts(key, t, tri_ref):
    eq = key == t
    eqb = jnp.where(eq, 1.0, 0.0).astype(BF16)
    stacked = jnp.concatenate([eqb[:, l * LANES:(l + 1) * LANES] for l in range(key.shape[1] // LANES)], axis=0)
    return eq, jnp.dot(stacked, tri_ref[...], preferred_element_type=F32)


def _tie_bias(key, t, eq, counts, need, run):
    rows, sc = key.shape
    before = []
    for l in range(sc // LANES):
        blk = counts[l * rows:(l + 1) * rows]
        before.append(run + blk[:, :LANES])
        run = run + blk[:, LANES:]
    sel = (key > t) | (eq & (jnp.concatenate(before, axis=1) < need))
    return jnp.where(sel, 0.0, NEG_BIG), run


def _bias_ties(key, t, need, run, tri_ref):
    eq, counts = _tie_counts(key, t, tri_ref)
    return _tie_bias(key, t, eq, counts, need, run)


def _pair_scores(qs_ref, kc):
    pairs = N_HEADS // 2
    prow = qs_ref.shape[0] // pairs
    return [lax.dot_general(qs_ref[g * prow:(g + 1) * prow, :], kc[:, g * LANES:(g + 1) * LANES], _NT,
                            preferred_element_type=F32) for g in range(pairs)]


def _softmax_pv(scores, bias, vc, m_ref, acc_ref):
    pairs = N_HEADS // 2
    nrep = vc.shape[0] // LANES
    bias2 = jnp.concatenate([bias, bias], axis=0)
    ones = jnp.ones((vc.shape[0], LANES), BF16)
    probs, alphas = [], []
    for g in range(pairs):
        s = scores[g] + bias2
        m_prev = m_ref[g]
        m_next = jnp.maximum(m_prev, jnp.max(s, axis=1, keepdims=True))
        alphas.append(jnp.exp2(m_prev - m_next))
        probs.append(jnp.exp2(s - jnp.concatenate([m_next] * nrep, axis=1)).astype(BF16))
        m_ref[g] = m_next
    for g in range(pairs):
        vext = jnp.concatenate([vc[:, g * LANES:(g + 1) * LANES], ones], axis=1)
        pv = jnp.dot(probs[g], vext, preferred_element_type=F32)
        acc_ref[g] = jnp.concatenate([alphas[g], alphas[g]], axis=1) * acc_ref[g] + pv


def _attend_chunk(qs_ref, bias, kc, vc, m_ref, acc_ref):
    _softmax_pv(_pair_scores(qs_ref, kc), bias, vc, m_ref, acc_ref)


def _init_attend(m_ref, acc_ref):
    m_ref[...] = jnp.full(m_ref.shape, NEG_BIG, F32)
    acc_ref[...] = jnp.zeros(acc_ref.shape, F32)


def _finish_attend(o_ref, acc_ref):
    rows = acc_ref.shape[1] // 2
    lane = lax.broadcasted_iota(I32, (rows, LANES), 1)
    for g in range(N_HEADS // 2):
        acc = acc_ref[g]
        o = acc[:, :LANES] / acc[:, LANES:]
        o_ref[:, g * LANES:(g + 1) * LANES] = jnp.where(lane < HEAD_DIM, o[:rows], o[rows:]).astype(o_ref.dtype)


def _prompt_attn_kernel(q_ref, iq_ref, iw_ref, kb_ref, vb_ref, ik2_ref, o_ref,
                        key_ref, iqs_ref, qs_ref, wrep_ref, m_ref, acc_ref, tri_ref, *, tq, sc, topk):
    j = pl.program_id(0)
    n_chunks = ((j + 1) * tq + sc - 1) // sc

    _init_tri(tri_ref)
    _stack_heads(iq_ref, iqs_ref, tq)
    _stack_heads(q_ref, qs_ref, tq)
    _spread_weights(iw_ref[...], wrep_ref)
    row = lax.broadcasted_iota(I32, (tq, 1), 0)
    limit = j * tq + (row // CHUNK + 1) * CHUNK

    def pairs_then_tail(fn, carry=0):
        carry = lax.fori_loop(0, n_chunks // 2, lambda i, carry: fn((2 * i, 2 * i + 1), carry), carry)

        @pl.when(n_chunks % 2 == 1)
        def _():
            fn((n_chunks - 1,), carry)

    def score_chunks(chunks, carry):
        offs = [pl.multiple_of(c * sc, sc) for c in chunks]
        dots = [_index_dots(iqs_ref, ik2_ref[pl.ds(off, sc), :]) for off in offs]
        for c, off, d in zip(chunks, offs, dots):
            sidx = off + lax.broadcasted_iota(I32, (tq, sc), 1)
            key_ref[c] = jnp.where(sidx < limit, _sortable(_index_combine(d, wrep_ref)), INT_MIN)
        return carry
    pairs_then_tail(score_chunks)

    def count_ge(cand):
        def body(c, acc):
            return acc + _count_tile(key_ref.at[c], cand)
        part = lax.fori_loop(0, n_chunks, body, jnp.zeros((tq, LANES), F32))
        return jnp.sum(part, axis=1, keepdims=True)

    def count_01():
        def body(c, acc):
            ge0, ge1 = _count_tile_01(key_ref.at[c])
            return acc[0] + ge0, acc[1] + ge1
        zero = jnp.zeros((tq, LANES), F32)
        ge0, ge1 = lax.fori_loop(0, n_chunks, body, (zero, zero))
        return jnp.sum(ge0, axis=1, keepdims=True), jnp.sum(ge1, axis=1, keepdims=True)

    t, n_ge = _select_threshold(count_ge, count_01, limit.astype(F32), topk)
    has_ties = jnp.max(n_ge) > topk

    @pl.when(jnp.logical_not(has_ties))
    def _():
        def body(c, carry):
            key_ref[c] = lax.bitcast_convert_type(_bias_plain(key_ref[c], t), I32)
            return carry
        lax.fori_loop(0, n_chunks, body, 0)

    @pl.when(has_ties)
    def _():
        need = topk - count_ge(t + 1)

        def tie_chunks(chunks, run):
            keys = [key_ref[c] for c in chunks]
            counted = [_tie_counts(key, t, tri_ref) for key in keys]
            for c, key, (eq, counts) in zip(chunks, keys, counted):
                bias, run = _tie_bias(key, t, eq, counts, need, run)
                key_ref[c] = lax.bitcast_convert_type(bias, I32)
            return run
        pairs_then_tail(tie_chunks, jnp.zeros((tq, LANES), F32))

    _init_attend(m_ref, acc_ref)

    def attend_chunks(chunks, carry):
        offs = [pl.multiple_of(c * sc, sc) for c in chunks]
        scores = [_pair_scores(qs_ref, kb_ref[pl.ds(off, sc), :]) for off in offs]
        for c, off, s in zip(chunks, offs, scores):
            _softmax_pv(s, lax.bitcast_convert_type(key_ref[c], F32), vb_ref[pl.ds(off, sc), :], m_ref, acc_ref)
        return carry
    pairs_then_tail(attend_chunks)
    _finish_attend(o_ref, acc_ref)


def _attn_scratch(rows):
    pairs = N_HEADS // 2
    return [
        pltpu.VMEM((N_HEADS * rows, LANES), BF16),
        pltpu.VMEM((N_HEADS * rows, LANES), BF16),
        pltpu.VMEM((IDX_HEADS, rows, LANES), F32),
        pltpu.VMEM((pairs, 2 * rows, LANES), F32),
        pltpu.VMEM((pairs, 2 * rows, 2 * LANES), F32),
        pltpu.VMEM((LANES, 2 * LANES), BF16),
    ]


def _prompt_attention(q, iq, iw, kb, vb, ik2, topk, tq=128, sc=512):
    s = kb.shape[0]
    row = lambda width: pl.BlockSpec((tq, width), lambda j: (j, 0))
    whole = pl.BlockSpec(memory_space=pltpu.VMEM)
    return pl.pallas_call(
        functools.partial(_prompt_attn_kernel, tq=tq, sc=sc, topk=topk),
        out_shape=jax.ShapeDtypeStruct((s, ATTN_DIM), BF16),
        grid=(s // tq,),
        in_specs=[row(ATTN_DIM), row(ATTN_DIM), row(LANES), whole, whole, whole],
        out_specs=row(ATTN_DIM),
        scratch_shapes=[pltpu.VMEM((s // sc, tq, sc), I32)] + _attn_scratch(tq),
        compiler_params=pltpu.CompilerParams(dimension_semantics=("arbitrary",),
                                             vmem_limit_bytes=VMEM_LIMIT),
        name="prompt_attention",
    )(q, iq, iw, kb, vb, ik2)


def _sample_attn_kernel(q_ref, iq_ref, iw_ref, kn_ref, vn_ref, ikn_ref, ck_ref, cv_ref, cik_ref, o_ref,
                        key_ref, keyn_ref, iqs_ref, qs_ref, wrep_ref, m_ref, acc_ref, tri_ref, *, sc, topk):
    ts = q_ref.shape[0]
    past = ck_ref.shape[1]
    n_chunks = past // sc

    _init_tri(tri_ref)
    _stack_heads(iq_ref, iqs_ref, ts)
    _stack_heads(q_ref, qs_ref, ts)
    _spread_weights(iw_ref[...], wrep_ref)

    pad = lambda x: jnp.concatenate([x, jnp.zeros((LANES - ts, x.shape[1]), x.dtype)], axis=0)
    new_ok = lax.broadcasted_iota(I32, (ts, LANES), 1) < ts

    def cache_ik(c):
        ik = cik_ref[0, c * sc:(c + 1) * sc, :].astype(BF16)
        return jnp.concatenate([ik, ik], axis=1)

    for c in range(n_chunks):
        key_ref[c] = _sortable(_index_scores(iqs_ref, wrep_ref, cache_ik(c)))
    keyn_ref[...] = jnp.where(new_ok, _sortable(_index_scores(iqs_ref, wrep_ref, pad(ikn_ref[...]))), INT_MIN)

    def count_ge(cand):
        part = jnp.where(keyn_ref[...] >= cand, 1.0, 0.0)
        for c in range(n_chunks):
            part = part + _count_tile(key_ref.at[c], cand)
        return jnp.sum(part, axis=1, keepdims=True)

    def count_01():
        ge0 = jnp.where(keyn_ref[...] >= 0, 1.0, 0.0)
        ge1 = jnp.where(keyn_ref[...] >= 1, 1.0, 0.0)
        for c in range(n_chunks):
            a0, a1 = _count_tile_01(key_ref.at[c])
            ge0, ge1 = ge0 + a0, ge1 + a1
        return jnp.sum(ge0, axis=1, keepdims=True), jnp.sum(ge1, axis=1, keepdims=True)

    t, n_ge = _select_threshold(count_ge, count_01, jnp.full((ts, 1), past + ts, F32), topk)
    has_ties = jnp.max(n_ge) > topk

    @pl.when(jnp.logical_not(has_ties))
    def _():
        for c in range(n_chunks):
            key_ref[c] = lax.bitcast_convert_type(_bias_plain(key_ref[c], t), I32)
        keyn_ref[...] = lax.bitcast_convert_type(_bias_plain(keyn_ref[...], t), I32)

    @pl.when(has_ties)
    def _():
        need = topk - count_ge(t + 1)
        run = jnp.zeros((ts, LANES), F32)
        for c in range(n_chunks):
            bias, run = _bias_ties(key_ref[c], t, need, run, tri_ref)
            key_ref[c] = lax.bitcast_convert_type(bias, I32)
        bias, run = _bias_ties(keyn_ref[...], t, need, run, tri_ref)
        keyn_ref[...] = lax.bitcast_convert_type(bias, I32)

    _init_attend(m_ref, acc_ref)
    for c in range(n_chunks):
        _attend_chunk(qs_ref,lax.bitcast_convert_type(key_ref[c], F32),
                      ck_ref[0, c * sc:(c + 1) * sc, :].astype(BF16),
                      cv_ref[0, c * sc:(c + 1) * sc, :].astype(BF16), m_ref, acc_ref)
    _attend_chunk(qs_ref,lax.bitcast_convert_type(keyn_ref[...], F32),
                  pad(kn_ref[...]), pad(vn_ref[...]), m_ref, acc_ref)
    _finish_attend(o_ref, acc_ref)


def _sample_attention(q, iq, iw, kb, vb, ik2, cache_k, cache_v, cache_ik, ts, topk, sc=512):
    t = q.shape[0]
    nb, past = cache_k.shape[:2]
    row = lambda width: pl.BlockSpec((ts, width), lambda b: (b, 0))
    cache = lambda width: pl.BlockSpec((1, past, width), lambda b: (b, 0, 0))
    return pl.pallas_call(
        functools.partial(_sample_attn_kernel, sc=sc, topk=topk),
        out_shape=jax.ShapeDtypeStruct((t, ATTN_DIM), BF16),
        grid=(nb,),
        in_specs=[row(ATTN_DIM), row(ATTN_DIM), row(LANES), row(ATTN_DIM), row(ATTN_DIM), row(LANES),
                  cache(ATTN_DIM), cache(ATTN_DIM), cache(IDX_DIM)],
        out_specs=row(ATTN_DIM),
        scratch_shapes=([pltpu.VMEM((past // sc, ts, sc), I32), pltpu.VMEM((ts, LANES), I32)]
                        + _attn_scratch(ts)),
        compiler_params=pltpu.CompilerParams(dimension_semantics=("arbitrary",),
                                             vmem_limit_bytes=VMEM_LIMIT),
        name="sample_attention",
    )(q, iq, iw, kb, vb, ik2, cache_k, cache_v, cache_ik)


HIST_ROW0 = 2


def _merge_kernel(x_ref, a_ref, hist_ref, g_ref, wglu_ref, wgate_ref, bgate_ref, wa_ref,
                  dww_ref, dwb_ref, lng_ref, lnb_ref, wc_ref, wo_ref, postg_ref,
                  y_ref, conv_o_ref, uext_ref, *, nseq, seqlen, carry):
    i = pl.program_id(0)
    x = x_ref[...]
    n = _rms(x, g_ref[...]).astype(BF16)
    glu = jnp.dot(n, wglu_ref[...], preferred_element_type=F32)
    u = glu[:, :CONV_CH] * _sigmoid(glu[:, CONV_CH:])

    new0 = HIST_ROW0 + HIST
    if carry:
        @pl.when(i == 0)
        def _():
            uext_ref[:, HIST_ROW0:new0, :] = hist_ref[...]

        @pl.when(i > 0)
        def _():
            uext_ref[:, HIST_ROW0:new0, :] = uext_ref[:, seqlen + HIST_ROW0:seqlen + new0, :]
    else:
        uext_ref[:, HIST_ROW0:new0, :] = hist_ref[...]
    uext_ref[:, new0:new0 + seqlen, :] = u.reshape(nseq, seqlen, CONV_CH)
    conv_o_ref[...] = uext_ref[:, seqlen + HIST_ROW0:seqlen + new0, :]

    dww = dww_ref[...]
    cdw = None
    for tap in range(CONV_WIDTH):
        term = uext_ref[:, HIST_ROW0 + tap:HIST_ROW0 + tap + seqlen, :] * dww[tap:tap + 1, :]
        cdw = term if cdw is None else cdw + term
    cdw = cdw.reshape(nseq * seqlen, CONV_CH) + dwb_ref[...]

    mu = jnp.mean(cdw, axis=-1, keepdims=True)
    xc = cdw - mu
    var = jnp.mean(xc * xc, axis=-1, keepdims=True)
    ln = xc * lax.rsqrt(var + EPS) * lng_ref[...] + lnb_ref[...]
    cact = (ln * _sigmoid(ln)).astype(BF16)
    c_proj = jnp.dot(cact, wc_ref[...], preferred_element_type=F32)
    a_proj = jnp.dot(a_ref[...], wa_ref[...], preferred_element_type=F32)

    d = x.shape[1]
    gate = _sigmoid(jnp.dot(n, wgate_ref[...], preferred_element_type=F32) + bgate_ref[...])
    merged = (gate[:, :d] * a_proj + gate[:, d:] * c_proj).astype(BF16)
    y = jnp.dot(merged, wo_ref[...], preferred_element_type=F32)
    y_ref[...] = x + _rms(y, postg_ref[...])


def _branch_merge(x, a, hist, g, wglu, wgate, bgate, wa, dww, dwb, lng, lnb, wc, wo, postg,
                  *, nseq, seqlen, carry):
    t, d = x.shape
    tm = nseq * seqlen
    row = lambda width: pl.BlockSpec((tm, width), lambda i: (i, 0))
    if carry:
        seq_spec = pl.BlockSpec((nseq, HIST, CONV_CH), lambda i: (0, 0, 0))
        n_state = nseq
    else:
        seq_spec = pl.BlockSpec((nseq, HIST, CONV_CH), lambda i: (i, 0, 0))
        n_state = t // seqlen
    consts = [g, wglu, wgate, bgate, wa, dww, dwb, lng, lnb, wc, wo, postg]
    return pl.pallas_call(
        functools.partial(_merge_kernel, nseq=nseq, seqlen=seqlen, carry=carry),
        out_shape=[jax.ShapeDtypeStruct((t, d), F32),
                   jax.ShapeDtypeStruct((n_state, HIST, CONV_CH), F32)],
        grid=(t // tm,),
        in_specs=[row(d), row(ATTN_DIM), seq_spec] + [_const_spec(c.shape) for c in consts],
        out_specs=[row(d), seq_spec],
        scratch_shapes=[pltpu.VMEM((nseq, HIST_ROW0 + HIST + seqlen, CONV_CH), F32)],
        compiler_params=pltpu.CompilerParams(dimension_semantics=("arbitrary",),
                                             vmem_limit_bytes=VMEM_LIMIT),
        name="branch_merge",
    )(x, a, hist, *consts)


def kernel(x_prompt, x_sample, cache_k, cache_v, cache_idx_k, state_conv,
           ffn1_pre_g, ffn1_w_gate, ffn1_w_up, ffn1_w_down, ffn1_post_g,
           mix_pre_g, w_in, b_gate, w_attn_o, conv_dw_w, conv_dw_b, conv_ln_g, conv_ln_b,
           w_conv_o, w_out, mix_post_g,
           ffn2_pre_g, ffn2_w_gate, ffn2_w_up, ffn2_w_down, ffn2_post_g):
    bp, lp, d = x_prompt.shape
    db, ts, _ = x_sample.shape
    depth, _, past = cache_k.shape[:3]
    assert bp == 1 and depth == 1
    topk_p = min(TOPK_MAX, lp // 4)
    topk_s = min(TOPK_MAX, (past + ts) // 4)

    vec = lambda a: a.reshape(1, -1)
    bf = lambda a: a.astype(BF16)
    ffn1 = (vec(ffn1_pre_g[0]), bf(ffn1_w_gate[0]), bf(ffn1_w_up[0]), bf(ffn1_w_down[0]), vec(ffn1_post_g[0]))
    ffn2 = (vec(ffn2_pre_g[0]), bf(ffn2_w_gate[0]), bf(ffn2_w_up[0]), bf(ffn2_w_down[0]), vec(ffn2_post_g[0]))

    widths = (IDX_HEADS * IDX_DIM, IDX_HEADS, IDX_DIM, ATTN_DIM, ATTN_DIM, ATTN_DIM, 2 * CONV_CH, 2 * d)
    offs = [0]
    for wd_ in widths:
        offs.append(offs[-1] + wd_)
    wi = w_in[0]
    col = lambda k: wi[:, offs[k]:offs[k + 1]]
    w_iw = jnp.pad(col(1), ((0, 0), (0, LANES - IDX_HEADS)))
    w2 = bf(jnp.concatenate([col(0), col(3), col(4), col(5), col(2), col(2), w_iw], axis=1))
    merge_w = (vec(mix_pre_g[0]), bf(col(6)), bf(col(7)), vec(b_gate[0]), bf(w_attn_o[0]),
               conv_dw_w[0], vec(conv_dw_b[0]), vec(conv_ln_g[0]), vec(conv_ln_b[0]),
               bf(w_conv_o[0]), bf(w_out[0]), vec(mix_post_g[0]))

    pos_p = jnp.arange(lp, dtype=I32)
    pos_s = jnp.tile(past + jnp.arange(ts, dtype=I32), db)

    hp = _ffn_half(x_prompt.reshape(lp, d), *ffn1)
    iq, q, kf, kb, vf, vb, ikf, ik2, iw = _mixer_proj(hp, vec(mix_pre_g[0]), w2, _rope_angles(pos_p))
    a = _prompt_attention(q, iq, iw, kb, vb, ik2, topk_p)
    hp, conv_p = _branch_merge(hp, a, jnp.zeros((1, HIST, CONV_CH), F32), *merge_w,
                               nseq=1, seqlen=512, carry=True)
    hp = _ffn_half(hp, *ffn2)
    outs_p = (kf.reshape(1, 1, lp, N_HEADS, HEAD_DIM), vf.reshape(1, 1, lp, N_HEADS, HEAD_DIM),
              ikf.reshape(1, 1, lp, IDX_DIM), conv_p.reshape(1, 1, HIST, CONV_CH))

    n_s = db * ts
    hs = _ffn_half(x_sample.reshape(n_s, d), *ffn1)
    iq, q, kf, kb, vf, vb, ikf, ik2, iw = _mixer_proj(hs, vec(mix_pre_g[0]), w2, _rope_angles(pos_s))
    a = _sample_attention(q, iq, iw, kb, vb, ik2, cache_k[0].reshape(db, past, ATTN_DIM),
                          cache_v[0].reshape(db, past, ATTN_DIM), cache_idx_k[0], ts, topk_s)
    hs, conv_s = _branch_merge(hs, a, state_conv[0], *merge_w, nseq=8, seqlen=ts, carry=False)
    hs = _ffn_half(hs, *ffn2)
    outs_s = (kf.reshape(1, db, ts, N_HEADS, HEAD_DIM), vf.reshape(1, db, ts, N_HEADS, HEAD_DIM),
              ikf.reshape(1, db, ts, IDX_DIM), conv_s.reshape(1, db, HIST, CONV_CH))

    return (hp.reshape(1, lp, d), hs.reshape(db, ts, d)) + outs_p + outs_s
```

```python
import functools
import math

import jax
import jax.numpy as jnp
from jax import lax
from jax.experimental import pallas as pl
from jax.experimental.pallas import tpu as pltpu

F32 = jnp.float32
BF16 = jnp.bfloat16
I32 = jnp.int32

CHUNK = 64
N_HEADS = 8
HEAD_DIM = 64
ATTN_DIM = N_HEADS * HEAD_DIM
IDX_HEADS = 8
IDX_DIM = 64
IDX_W_SCALE = (IDX_HEADS * IDX_DIM) ** -0.5
TOPK_MAX = 256
ROPE_THETA = 500000.0
ROT = HEAD_DIM // 4
CONV_CH = 512
CONV_WIDTH = 31
HIST = CONV_WIDTH - 1
EPS = 1e-6

LANES = 128
VMEM_LIMIT = 56 * 1024 * 1024
INT_MIN = -(2 ** 31)
NEG_BIG = -1e30
Q_SCALE = HEAD_DIM ** -0.5 * math.log2(math.e)

_NT = (((1,), (1,)), ((), ()))


def _rms(x, g):
    return x * lax.rsqrt(jnp.mean(x * x, axis=-1, keepdims=True) + EPS) * g


def _sigmoid(x):
    return 1.0 / (1.0 + jnp.exp(-x))


def _const_spec(shape):
    nd = len(shape)
    return pl.BlockSpec(shape, lambda *_: (0,) * nd, pipeline_mode=pl.Buffered(1))


def _ffn_kernel(x_ref, pre_g_ref, wg_ref, wu_ref, wd_ref, post_g_ref, o_ref):
    x = x_ref[...]
    h = _rms(x, pre_g_ref[...]).astype(BF16)
    g = jnp.dot(h, wg_ref[...], preferred_element_type=F32)
    u = jnp.dot(h, wu_ref[...], preferred_element_type=F32)
    a = (g * _sigmoid(g) * u).astype(BF16)
    f = jnp.dot(a, wd_ref[...], preferred_element_type=F32)
    o_ref[...] = x + 0.5 * _rms(f, post_g_ref[...])


def _ffn_half(x, pre_g, wg, wu, wd, post_g, tm=512):
    t, d = x.shape
    dff = wg.shape[1]
    row = pl.BlockSpec((tm, d), lambda i: (i, 0))
    return pl.pallas_call(
        _ffn_kernel,
        out_shape=jax.ShapeDtypeStruct((t, d), F32),
        grid=(t // tm,),
        in_specs=[row, _const_spec((1, d)), _const_spec((d, dff)), _const_spec((d, dff)),
                  _const_spec((dff, d)), _const_spec((1, d))],
        out_specs=row,
        compiler_params=pltpu.CompilerParams(dimension_semantics=("parallel",),
                                             vmem_limit_bytes=VMEM_LIMIT),
        name="ffn_half",
    )(x, pre_g, wg, wu, wd, post_g)


def _rope_angles(pos):
    half = ROT // 2
    inv_freq = jnp.exp(-math.log(ROPE_THETA) * jnp.arange(half, dtype=F32) * (2.0 / ROT))
    ang = pos.astype(F32)[:, None] * inv_freq[None, :]
    return jnp.pad(jnp.concatenate([jnp.cos(ang), jnp.sin(ang)], axis=1), ((0, 0), (0, LANES - ROT)))


def _rope_tables(cs):
    half = ROT // 2
    lane = lax.broadcasted_iota(I32, cs.shape, 1)
    both = lambda a: a + pltpu.roll(a, HEAD_DIM, 1)
    cos_lo = jnp.where(lane < half, cs, 0.0)
    sin_hi = jnp.where(jnp.logical_and(lane >= half, lane < ROT), cs, 0.0)
    c = both(cos_lo + pltpu.roll(cos_lo, half, 1))
    c = jnp.where(lane % HEAD_DIM < ROT, c, 1.0)
    s_dn = both(sin_hi)
    s_up = -both(pltpu.roll(sin_hi, LANES - half, 1))
    return c, s_up, s_dn


def _proj_kernel(x_ref, g_ref, w_ref, cs_ref,
                 iq_o, q_o, kf_o, kb_o, vf_o, vb_o, ikf_o, ik2_o, iw_o):
    n = _rms(x_ref[...], g_ref[...]).astype(BF16)
    p = jnp.dot(n, w_ref[...], preferred_element_type=F32)
    c, s_up, s_dn = _rope_tables(cs_ref[...])

    def rope(x):
        return x * c + pltpu.roll(x, LANES - ROT // 2, 1) * s_up + pltpu.roll(x, ROT // 2, 1) * s_dn

    a = ATTN_DIM
    for gidx in range(a // LANES):
        sl = slice(gidx * LANES, (gidx + 1) * LANES)
        iq_o[:, sl] = rope(p[:, sl]).astype(BF16)
        q_o[:, sl] = (rope(p[:, a + gidx * LANES:a + (gidx + 1) * LANES]) * Q_SCALE).astype(BF16)
        k = rope(p[:, 2 * a + gidx * LANES:2 * a + (gidx + 1) * LANES])
        kf_o[:, sl] = k
        kb_o[:, sl] = k.astype(BF16)
        v = p[:, 3 * a + gidx * LANES:3 * a + (gidx + 1) * LANES]
        vf_o[:, sl] = v
        vb_o[:, sl] = v.astype(BF16)
    ik = rope(p[:, 4 * a:4 * a + LANES])
    ikf_o[...] = ik[:, :IDX_DIM]
    ik2_o[...] = ik.astype(BF16)
    iw_o[...] = p[:, 4 * a + LANES:4 * a + 2 * LANES] * IDX_W_SCALE


def _ffn_proj_kernel(x_ref, pre_g_ref, wg_ref, wu_ref, wd_ref, post_g_ref, g_ref, w_ref, cs_ref, y_o, *proj_outs):
    _ffn_kernel(x_ref, pre_g_ref, wg_ref, wu_ref, wd_ref, post_g_ref, y_o)
    _proj_kernel(y_o, g_ref, w_ref, cs_ref, *proj_outs)


def _mixer_proj(x, g, w2, angles, ffn=None, tm=512):
    t, d = x.shape
    ncol = w2.shape[1]
    row = lambda width: pl.BlockSpec((tm, width), lambda i: (i, 0))
    a = ATTN_DIM
    outs = [
        jax.ShapeDtypeStruct((t, a), BF16),
        jax.ShapeDtypeStruct((t, a), BF16),
        jax.ShapeDtypeStruct((t, a), F32),
        jax.ShapeDtypeStruct((t, a), BF16),
        jax.ShapeDtypeStruct((t, a), F32),
        jax.ShapeDtypeStruct((t, a), BF16),
        jax.ShapeDtypeStruct((t, IDX_DIM), F32),
        jax.ShapeDtypeStruct((t, LANES), BF16),
        jax.ShapeDtypeStruct((t, LANES), F32),
    ]
    proj_in = [_const_spec((1, d)), _const_spec((d, ncol)), row(LANES)]
    proj_out = [row(a), row(a), row(a), row(a), row(a), row(a), row(IDX_DIM), row(LANES), row(LANES)]
    params = pltpu.CompilerParams(dimension_semantics=("parallel",), vmem_limit_bytes=VMEM_LIMIT)
    if ffn is None:
        return pl.pallas_call(
            _proj_kernel, out_shape=outs, grid=(t // tm,), in_specs=[row(d)] + proj_in, out_specs=proj_out,
            compiler_params=params, name="mixer_proj",
        )(x, g, w2, angles)
    dff = ffn[1].shape[1]
    ffn_in = [_const_spec((1, d)), _const_spec((d, dff)), _const_spec((d, dff)), _const_spec((dff, d)),
              _const_spec((1, d))]
    return pl.pallas_call(
        _ffn_proj_kernel, out_shape=[jax.ShapeDtypeStruct((t, d), F32)] + outs, grid=(t // tm,),
        in_specs=[row(d)] + ffn_in + proj_in, out_specs=[row(d)] + proj_out,
        compiler_params=params, name="ffn_mixer_proj",
    )(x, *ffn, g, w2, angles)


NO_KEY = float("-inf")


def _key_to_score(k):
    return lax.bitcast_convert_type(k ^ ((k >> 31) & 0x7FFFFFFF), F32)


def _head_lhs(x, h):
    grp, odd = divmod(h, 2)
    xg = x[:, grp * LANES:(grp + 1) * LANES].astype(F32)
    lane = lax.broadcasted_iota(I32, xg.shape, 1)
    keep = (lane >= HEAD_DIM) if odd else (lane < HEAD_DIM)
    return jnp.where(keep, xg, 0.0).astype(BF16)


def _stack_heads(x_ref, dst_ref, rows):
    x = x_ref[...]
    for h in range(N_HEADS):
        dst_ref[h * rows:(h + 1) * rows, :] = _head_lhs(x, h)


def _spread_weights(w, wrep_ref):
    for h in range(IDX_HEADS):
        wrep_ref[h] = jnp.broadcast_to(w[:, h:h + 1], wrep_ref.shape[1:])


def _index_dots(iqs_ref, ikc):
    return lax.dot_general(iqs_ref[...], ikc, _NT, preferred_element_type=F32)


def _index_combine(d, wrep_ref):
    rows = wrep_ref.shape[1]
    groups = []
    for l in range(d.shape[1] // LANES):
        acc = None
        for h in range(IDX_HEADS):
            term = jnp.maximum(d[h * rows:(h + 1) * rows, l * LANES:(l + 1) * LANES], 0.0) * wrep_ref[h]
            acc = term if acc is None else acc + term
        groups.append(acc)
    return jnp.concatenate(groups, axis=1)


def _index_scores(iqs_ref, wrep_ref, ikc):
    return _index_combine(_index_dots(iqs_ref, ikc), wrep_ref)


def _lane_fold(x):
    out = x[:, :LANES]
    for l in range(1, x.shape[1] // LANES):
        out = out + x[:, l * LANES:(l + 1) * LANES]
    return out


COUNT_ROWS = 32


def _count_tile(tile_ref, cand, strict):
    rows = tile_ref.shape[0]
    step = min(COUNT_ROWS, rows)
    parts = []
    for r in range(0, rows, step):
        tile, c = tile_ref[r:r + step, :], cand[r:r + step]
        hit = jnp.where(tile > c if strict else tile >= c, 1.0, 0.0)
        parts.append(_lane_fold(hit))
    return jnp.concatenate(parts, axis=0)


def _select_threshold(count, n_adm, topk):
    zero = jnp.zeros(n_adm.shape, F32)
    c0 = count(zero, False)
    c1 = count(zero, True)
    nonneg = c0 >= topk
    zero_tie = jnp.logical_and(nonneg, c1 < topk)

    def cond(st):
        i, _, cnt = st
        return jnp.logical_and(i < 32, jnp.max(jnp.where(zero_tie, 0.0, cnt)) > topk)

    def one_bit(i, t, cnt):
        cand = t + jnp.left_shift(jnp.int32(1), 31 - i)
        c = count(_key_to_score(cand), False)
        ok = c >= topk
        return jnp.where(ok, cand, t), jnp.where(ok, c, cnt)

    def step(st):
        i, t, cnt = st
        t, cnt = one_bit(i, t, cnt)
        t, cnt = one_bit(i + 1, t, cnt)
        return i + 2, t, cnt

    t, cnt = one_bit(1, jnp.where(nonneg, 0, INT_MIN), jnp.where(nonneg, c0, n_adm))
    _, t, cnt = lax.while_loop(cond, step, (jnp.int32(2), t, cnt))
    return jnp.where(t == INT_MIN, float(jnp.finfo(F32).min), _key_to_score(t)), cnt


def _bias_plain(score, t):
    return jnp.where(score >= t, 0.0, NEG_BIG)


def _init_tri(tri_ref):
    @pl.when(pl.program_id(0) == 0)
    def _():
        r = lax.broadcasted_iota(I32, tri_ref.shape, 0)
        c = lax.broadcasted_iota(I32, tri_ref.shape, 1)
        tri_ref[...] = jnp.where(jnp.logical_or(r < c, c >= LANES), 1.0, 0.0).astype(BF16)


def _tie_counts(key, t, tri_ref):
    eq = key == t
    eqb = jnp.where(eq, 1.0, 0.0).astype(BF16)
    stacked = jnp.concatenate([eqb[:, l * LANES:(l + 1) * LANES] for l in range(key.shape[1] // LANES)], axis=0)
    return eq, jnp.dot(stacked, tri_ref[...], preferred_element_type=F32)


def _tie_bias(key, t, eq, counts, need, run):
    rows, sc = key.shape
    before = []
    for l in range(sc // LANES):
        blk = counts[l * rows:(l + 1) * rows]
        before.append(run + blk[:, :LANES])
        run = run + blk[:, LANES:]
    sel = (key > t) | (eq & (jnp.concatenate(before, axis=1) < need))
    return jnp.where(sel, 0.0, NEG_BIG), run


def _bias_ties(key, t, need, run, tri_ref):
    eq, counts = _tie_counts(key, t, tri_ref)
    return _tie_bias(key, t, eq, counts, need, run)


def _pair_scores(qs_ref, kc):
    pairs = N_HEADS // 2
    prow = qs_ref.shape[0] // pairs
    return [lax.dot_general(qs_ref[g * prow:(g + 1) * prow, :], kc[:, g * LANES:(g + 1) * LANES], _NT,
                            preferred_element_type=F32) for g in range(pairs)]


def _softmax_pv(scores, bias, vc, m_ref, acc_ref):
    pairs = N_HEADS // 2
    nrep = vc.shape[0] // LANES
    bias2 = jnp.concatenate([bias, bias], axis=0)
    ones = jnp.ones((vc.shape[0], LANES), BF16)
    probs, alphas = [], []
    for g in range(pairs):
        s = scores[g] + bias2
        m_prev = m_ref[g]
        m_next = jnp.maximum(m_prev, jnp.max(s, axis=1, keepdims=True))
        alphas.append(jnp.exp2(m_prev - m_next))
        probs.append(jnp.exp2(s - jnp.concatenate([m_next] * nrep, axis=1)).astype(BF16))
        m_ref[g] = m_next
    for g in range(pairs):
        vext = jnp.concatenate([vc[:, g * LANES:(g + 1) * LANES], ones], axis=1)
        pv = jnp.dot(probs[g], vext, preferred_element_type=F32)
        acc_ref[g] = jnp.concatenate([alphas[g], alphas[g]], axis=1) * acc_ref[g] + pv


def _attend_chunk(qs_ref, bias, kc, vc, m_ref, acc_ref):
    _softmax_pv(_pair_scores(qs_ref, kc), bias, vc, m_ref, acc_ref)


def _init_attend(m_ref, acc_ref):
    m_ref[...] = jnp.full(m_ref.shape, NEG_BIG, F32)
    acc_ref[...] = jnp.zeros(acc_ref.shape, F32)


def _finish_attend(o_ref, acc_ref):
    rows = acc_ref.shape[1] // 2
    lane = lax.broadcasted_iota(I32, (rows, LANES), 1)
    for g in range(N_HEADS // 2):
        acc = acc_ref[g]
        o = acc[:, :LANES] / acc[:, LANES:]
        o_ref[:, g * LANES:(g + 1) * LANES] = jnp.where(lane < HEAD_DIM, o[:rows], o[rows:]).astype(o_ref.dtype)


def _prompt_attn_kernel(q_ref, iq_ref, iw_ref, kb_ref, vb_ref, ik2_ref, o_ref,
                        key_ref, iqs_ref, qs_ref, wrep_ref, m_ref, acc_ref, tri_ref, *, tq, sc, topk):
    j = pl.program_id(0)
    n_chunks = ((j + 1) * tq + sc - 1) // sc

    _init_tri(tri_ref)
    _stack_heads(iq_ref, iqs_ref, tq)
    _stack_heads(q_ref, qs_ref, tq)
    _spread_weights(iw_ref[...], wrep_ref)
    row = lax.broadcasted_iota(I32, (tq, 1), 0)
    limit = j * tq + (row // CHUNK + 1) * CHUNK

    def pairs_then_tail(fn, carry=0):
        carry = lax.fori_loop(0, n_chunks // 2, lambda i, carry: fn((2 * i, 2 * i + 1), carry), carry)

        @pl.when(n_chunks % 2 == 1)
        def _():
            fn((n_chunks - 1,), carry)

    def score_chunks(chunks, carry):
        offs = [pl.multiple_of(c * sc, sc) for c in chunks]
        dots = [_index_dots(iqs_ref, ik2_ref[pl.ds(off, sc), :]) for off in offs]
        for c, off, d in zip(chunks, offs, dots):
            sidx = off + lax.broadcasted_iota(I32, (tq, sc), 1)
            key_ref[c] = jnp.where(sidx < limit, _index_combine(d, wrep_ref), NO_KEY)
        return carry
    pairs_then_tail(score_chunks)

    def count(cand, strict):
        def body(c, acc):
            return acc + _count_tile(key_ref.at[c], cand, strict)
        part = lax.fori_loop(0, n_chunks, body, jnp.zeros((tq, LANES), F32))
        return jnp.sum(part, axis=1, keepdims=True)

    t, n_ge = _select_threshold(count, limit.astype(F32), topk)
    has_ties = jnp.max(n_ge) > topk

    @pl.when(jnp.logical_not(has_ties))
    def _():
        def body(c, carry):
            key_ref[c] = _bias_plain(key_ref[c], t)
            return carry
        lax.fori_loop(0, n_chunks, body, 0)

    @pl.when(has_ties)
    def _():
        need = topk - count(t, True)

        def tie_chunks(chunks, run):
            keys = [key_ref[c] for c in chunks]
            counted = [_tie_counts(key, t, tri_ref) for key in keys]
            for c, key, (eq, counts) in zip(chunks, keys, counted):
                key_ref[c], run = _tie_bias(key, t, eq, counts, need, run)
            return run
        pairs_then_tail(tie_chunks, jnp.zeros((tq, LANES), F32))

    _init_attend(m_ref, acc_ref)

    def attend_chunks(chunks, carry):
        offs = [pl.multiple_of(c * sc, sc) for c in chunks]
        scores = [_pair_scores(qs_ref, kb_ref[pl.ds(off, sc), :]) for off in offs]
        for c, off, s in zip(chunks, offs, scores):
            _softmax_pv(s, key_ref[c], vb_ref[pl.ds(off, sc), :], m_ref, acc_ref)
        return carry
    pairs_then_tail(attend_chunks)
    _finish_attend(o_ref, acc_ref)


def _attn_scratch(rows):
    pairs = N_HEADS // 2
    return [
        pltpu.VMEM((N_HEADS * rows, LANES), BF16),
        pltpu.VMEM((N_HEADS * rows, LANES), BF16),
        pltpu.VMEM((IDX_HEADS, rows, LANES), F32),
        pltpu.VMEM((pairs, 2 * rows, LANES), F32),
        pltpu.VMEM((pairs, 2 * rows, 2 * LANES), F32),
        pltpu.VMEM((LANES, 2 * LANES), BF16),
    ]


def _prompt_attention(q, iq, iw, kb, vb, ik2, topk, tq=128, sc=512):
    s = kb.shape[0]
    row = lambda width: pl.BlockSpec((tq, width), lambda j: (j, 0))
    whole = pl.BlockSpec(memory_space=pltpu.VMEM)
    return pl.pallas_call(
        functools.partial(_prompt_attn_kernel, tq=tq, sc=sc, topk=topk),
        out_shape=jax.ShapeDtypeStruct((s, ATTN_DIM), BF16),
        grid=(s // tq,),
        in_specs=[row(ATTN_DIM), row(ATTN_DIM), row(LANES), whole, whole, whole],
        out_specs=row(ATTN_DIM),
        scratch_shapes=[pltpu.VMEM((s // sc, tq, sc), F32)] + _attn_scratch(tq),
        compiler_params=pltpu.CompilerParams(dimension_semantics=("arbitrary",),
                                             vmem_limit_bytes=VMEM_LIMIT),
        name="prompt_attention",
    )(q, iq, iw, kb, vb, ik2)


def _sample_attn_kernel(q_ref, iq_ref, iw_ref, kn_ref, vn_ref, ikn_ref, ck_ref, cv_ref, cik_ref, o_ref,
                        key_ref, keyn_ref, iqs_ref, qs_ref, wrep_ref, m_ref, acc_ref, tri_ref, *, sc, topk):
    ts = q_ref.shape[0]
    past = ck_ref.shape[1]
    n_chunks = past // sc

    _init_tri(tri_ref)
    _stack_heads(iq_ref, iqs_ref, ts)
    _stack_heads(q_ref, qs_ref, ts)
    _spread_weights(iw_ref[...], wrep_ref)

    pad = lambda x: jnp.concatenate([x, jnp.zeros((LANES - ts, x.shape[1]), x.dtype)], axis=0)
    new_ok = lax.broadcasted_iota(I32, (ts, LANES), 1) < ts

    def cache_ik(c):
        ik = cik_ref[0, c * sc:(c + 1) * sc, :].astype(BF16)
        return jnp.concatenate([ik, ik], axis=1)

    for c in range(n_chunks):
        key_ref[c] = _index_scores(iqs_ref, wrep_ref, cache_ik(c))
    keyn_ref[...] = jnp.where(new_ok, _index_scores(iqs_ref, wrep_ref, pad(ikn_ref[...])), NO_KEY)

    def count(cand, strict):
        part = _count_tile(keyn_ref, cand, strict)
        for c in range(n_chunks):
            part = part + _count_tile(key_ref.at[c], cand, strict)
        return jnp.sum(part, axis=1, keepdims=True)

    t, n_ge = _select_threshold(count, jnp.full((ts, 1), past + ts, F32), topk)
    has_ties = jnp.max(n_ge) > topk

    @pl.when(jnp.logical_not(has_ties))
    def _():
        for c in range(n_chunks):
            key_ref[c] = _bias_plain(key_ref[c], t)
        keyn_ref[...] = _bias_plain(keyn_ref[...], t)

    @pl.when(has_ties)
    def _():
        need = topk - count(t, True)
        run = jnp.zeros((ts, LANES), F32)
        for c in range(n_chunks):
            key_ref[c], run = _bias_ties(key_ref[c], t, need, run, tri_ref)
        keyn_ref[...], run = _bias_ties(keyn_ref[...], t, need, run, tri_ref)

    _init_attend(m_ref, acc_ref)
    for c in range(n_chunks):
        _attend_chunk(qs_ref, key_ref[c], ck_ref[0, c * sc:(c + 1) * sc, :].astype(BF16),
                      cv_ref[0, c * sc:(c + 1) * sc, :].astype(BF16), m_ref, acc_ref)
    _attend_chunk(qs_ref, keyn_ref[...], pad(kn_ref[...]), pad(vn_ref[...]), m_ref, acc_ref)
    _finish_attend(o_ref, acc_ref)


def _sample_attention(q, iq, iw, kb, vb, ik2, cache_k, cache_v, cache_ik, ts, topk, sc=512):
    t = q.shape[0]
    nb, past = cache_k.shape[:2]
    row = lambda width: pl.BlockSpec((ts, width), lambda b: (b, 0))
    cache = lambda width: pl.BlockSpec((1, past, width), lambda b: (b, 0, 0))
    return pl.pallas_call(
        functools.partial(_sample_attn_kernel, sc=sc, topk=topk),
        out_shape=jax.ShapeDtypeStruct((t, ATTN_DIM), BF16),
        grid=(nb,),
        in_specs=[row(ATTN_DIM), row(ATTN_DIM), row(LANES), row(ATTN_DIM), row(ATTN_DIM), row(LANES),
                  cache(ATTN_DIM), cache(ATTN_DIM), cache(IDX_DIM)],
        out_specs=row(ATTN_DIM),
        scratch_shapes=([pltpu.VMEM((past // sc, ts, sc), F32), pltpu.VMEM((ts, LANES), F32)]
                        + _attn_scratch(ts)),
        compiler_params=pltpu.CompilerParams(dimension_semantics=("arbitrary",),
                                             vmem_limit_bytes=VMEM_LIMIT),
        name="sample_attention",
    )(q, iq, iw, kb, vb, ik2, cache_k, cache_v, cache_ik)


HIST_ROW0 = 2


def _merge_kernel(x_ref, a_ref, hist_ref, g_ref, wglu_ref, wgate_ref, bgate_ref, wa_ref,
                  dww_ref, dwb_ref, lng_ref, lnb_ref, wc_ref, wo_ref, postg_ref,
                  y_ref, conv_o_ref, uext_ref, *, nseq, seqlen, carry):
    i = pl.program_id(0)
    x = x_ref[...]
    n = _rms(x, g_ref[...]).astype(BF16)
    glu = jnp.dot(n, wglu_ref[...], preferred_element_type=F32)
    u = glu[:, :CONV_CH] * _sigmoid(glu[:, CONV_CH:])

    new0 = HIST_ROW0 + HIST
    if carry:
        @pl.when(i == 0)
        def _():
            uext_ref[:, HIST_ROW0:new0, :] = hist_ref[...]

        @pl.when(i > 0)
        def _():
            uext_ref[:, HIST_ROW0:new0, :] = uext_ref[:, seqlen + HIST_ROW0:seqlen + new0, :]
    else:
        uext_ref[:, HIST_ROW0:new0, :] = hist_ref[...]
    uext_ref[:, new0:new0 + seqlen, :] = u.reshape(nseq, seqlen, CONV_CH)
    conv_o_ref[...] = uext_ref[:, seqlen + HIST_ROW0:seqlen + new0, :]

    dww = dww_ref[...]
    cdw = None
    for tap in range(CONV_WIDTH):
        term = uext_ref[:, HIST_ROW0 + tap:HIST_ROW0 + tap + seqlen, :] * dww[tap:tap + 1, :]
        cdw = term if cdw is None else cdw + term
    cdw = cdw.reshape(nseq * seqlen, CONV_CH) + dwb_ref[...]

    mu = jnp.mean(cdw, axis=-1, keepdims=True)
    xc = cdw - mu
    var = jnp.mean(xc * xc, axis=-1, keepdims=True)
    ln = xc * lax.rsqrt(var + EPS) * lng_ref[...] + lnb_ref[...]
    cact = (ln * _sigmoid(ln)).astype(BF16)
    c_proj = jnp.dot(cact, wc_ref[...], preferred_element_type=F32)
    a_proj = jnp.dot(a_ref[...], wa_ref[...], preferred_element_type=F32)

    d = x.shape[1]
    gate = _sigmoid(jnp.dot(n, wgate_ref[...], preferred_element_type=F32) + bgate_ref[...])
    merged = (gate[:, :d] * a_proj + gate[:, d:] * c_proj).astype(BF16)
    y = jnp.dot(merged, wo_ref[...], preferred_element_type=F32)
    y_ref[...] = x + _rms(y, postg_ref[...])


def _branch_merge(x, a, hist, g, wglu, wgate, bgate, wa, dww, dwb, lng, lnb, wc, wo, postg,
                  *, nseq, seqlen, carry):
    t, d = x.shape
    tm = nseq * seqlen
    row = lambda width: pl.BlockSpec((tm, width), lambda i: (i, 0))
    if carry:
        seq_spec = pl.BlockSpec((nseq, HIST, CONV_CH), lambda i: (0, 0, 0))
        n_state = nseq
    else:
        seq_spec = pl.BlockSpec((nseq, HIST, CONV_CH), lambda i: (i, 0, 0))
        n_state = t // seqlen
    consts = [g, wglu, wgate, bgate, wa, dww, dwb, lng, lnb, wc, wo, postg]
    return pl.pallas_call(
        functools.partial(_merge_kernel, nseq=nseq, seqlen=seqlen, carry=carry),
        out_shape=[jax.ShapeDtypeStruct((t, d), F32),
                   jax.ShapeDtypeStruct((n_state, HIST, CONV_CH), F32)],
        grid=(t // tm,),
        in_specs=[row(d), row(ATTN_DIM), seq_spec] + [_const_spec(c.shape) for c in consts],
        out_specs=[row(d), seq_spec],
        scratch_shapes=[pltpu.VMEM((nseq, HIST_ROW0 + HIST + seqlen, CONV_CH), F32)],
        compiler_params=pltpu.CompilerParams(dimension_semantics=("arbitrary",),
                                             vmem_limit_bytes=VMEM_LIMIT),
        name="branch_merge",
    )(x, a, hist, *consts)


def kernel(x_prompt, x_sample, cache_k, cache_v, cache_idx_k, state_conv,
           ffn1_pre_g, ffn1_w_gate, ffn1_w_up, ffn1_w_down, ffn1_post_g,
           mix_pre_g, w_in, b_gate, w_attn_o, conv_dw_w, conv_dw_b, conv_ln_g, conv_ln_b,
           w_conv_o, w_out, mix_post_g,
           ffn2_pre_g, ffn2_w_gate, ffn2_w_up, ffn2_w_down, ffn2_post_g):
    bp, lp, d = x_prompt.shape
    db, ts, _ = x_sample.shape
    depth, _, past = cache_k.shape[:3]
    assert bp == 1 and depth == 1
    topk_p = min(TOPK_MAX, lp // 4)
    topk_s = min(TOPK_MAX, (past + ts) // 4)

    vec = lambda a: a.reshape(1, -1)
    bf = lambda a: a.astype(BF16)
    ffn1 = (vec(ffn1_pre_g[0]), bf(ffn1_w_gate[0]), bf(ffn1_w_up[0]), bf(ffn1_w_down[0]), vec(ffn1_post_g[0]))
    ffn2 = (vec(ffn2_pre_g[0]), bf(ffn2_w_gate[0]), bf(ffn2_w_up[0]), bf(ffn2_w_down[0]), vec(ffn2_post_g[0]))

    widths = (IDX_HEADS * IDX_DIM, IDX_HEADS, IDX_DIM, ATTN_DIM, ATTN_DIM, ATTN_DIM, 2 * CONV_CH, 2 * d)
    offs = [0]
    for wd_ in widths:
        offs.append(offs[-1] + wd_)
    wi = w_in[0]
    col = lambda k: wi[:, offs[k]:offs[k + 1]]
    w_iw = jnp.pad(col(1), ((0, 0), (0, LANES - IDX_HEADS)))
    w2 = bf(jnp.concatenate([col(0), col(3), col(4), col(5), col(2), col(2), w_iw], axis=1))
    merge_w = (vec(mix_pre_g[0]), bf(col(6)), bf(col(7)), vec(b_gate[0]), bf(w_attn_o[0]),
               conv_dw_w[0], vec(conv_dw_b[0]), vec(conv_ln_g[0]), vec(conv_ln_b[0]),
               bf(w_conv_o[0]), bf(w_out[0]), vec(mix_post_g[0]))

    pos_p = jnp.arange(lp, dtype=I32)
    pos_s = jnp.tile(past + jnp.arange(ts, dtype=I32), db)

    hp, iq, q, kf, kb, vf, vb, ikf, ik2, iw = _mixer_proj(x_prompt.reshape(lp, d), vec(mix_pre_g[0]), w2,
                                                          _rope_angles(pos_p), ffn=ffn1)
    a = _prompt_attention(q, iq, iw, kb, vb, ik2, topk_p)
    hp, conv_p = _branch_merge(hp, a, jnp.zeros((1, HIST, CONV_CH), F32), *merge_w,
                               nseq=1, seqlen=512, carry=True)
    hp = _ffn_half(hp, *ffn2)
    outs_p = (kf.reshape(1, 1, lp, N_HEADS, HEAD_DIM), vf.reshape(1, 1, lp, N_HEADS, HEAD_DIM),
              ikf.reshape(1, 1, lp, IDX_DIM), conv_p.reshape(1, 1, HIST, CONV_CH))

    n_s = db * ts
    hs, iq, q, kf, kb, vf, vb, ikf, ik2, iw = _mixer_proj(x_sample.reshape(n_s, d), vec(mix_pre_g[0]), w2,
                                                          _rope_angles(pos_s), ffn=ffn1)
    a = _sample_attention(q, iq, iw, kb, vb, ik2, cache_k[0].reshape(db, past, ATTN_DIM),
                          cache_v[0].reshape(db, past, ATTN_DIM), cache_idx_k[0], ts, topk_s)
    hs, conv_s = _branch_merge(hs, a, state_conv[0], *merge_w, nseq=8, seqlen=ts, carry=False)
    hs = _ffn_half(hs, *ffn2)
    outs_s = (kf.reshape(1, db, ts, N_HEADS, HEAD_DIM), vf.reshape(1, db, ts, N_HEADS, HEAD_DIM),
              ikf.reshape(1, db, ts, IDX_DIM), conv_s.reshape(1, db, HIST, CONV_CH))

    return (hp.reshape(1, lp, d), hs.reshape(db, ts, d)) + outs_p + outs_s
```
